```python
import jax
import jax.numpy as jnp
from jax import lax
import numpy as np

D_MODEL = 1024
BATCH = 8
SEQ = 4096
DEPTH = 4

CTX_LEN = 256
GRID_W = 64
MIX_WIDTH = D_MODEL
CONV_WIDTH = MIX_WIDTH // 2
RG_WIDTH = MIX_WIDTH - CONV_WIDTH
CONV_K = 31
CONV_PAD = (CONV_K - 1) // 2
RG_CONV_K = 4
RG_PAD_L = RG_CONV_K // 2
RG_PAD_R = RG_CONV_K - 1 - RG_PAD_L
RG_HEADS = 8
RG_HEAD_DIM = RG_WIDTH // RG_HEADS
RG_C = 8.0
FFN_DIM = ((11 * D_MODEL // 4 + 127) // 128) * 128
FFN_CONV_K = 3
N_MOD = 6
LN_EPS = 1e-5
DEEPNORM_ALPHA = (2.0 * DEPTH) ** 0.25
DEEPNORM_BETA = (8.0 * DEPTH) ** -0.25
COL_CONV_GATE = CONV_WIDTH
COL_RG_X = 2 * CONV_WIDTH
COL_RG_GATE = 2 * CONV_WIDTH + RG_WIDTH
IN_COLS = 2 * CONV_WIDTH + 2 * RG_WIDTH

kernel_name = "hybrid_conformer_rglru_dit_trunk"


def _layer_norm(x, g, b):
    xf = x.astype(jnp.float32)
    mu = jnp.mean(xf, axis=-1, keepdims=True)
    var = jnp.mean(jnp.square(xf - mu), axis=-1, keepdims=True)
    y = (xf - mu) * lax.rsqrt(var + LN_EPS) * g.astype(jnp.float32) + b.astype(jnp.float32)
    return y.astype(x.dtype)


def _modulate(h, shift, scale):
    return h * (1.0 + scale) + shift


def _dwconv1d(x, w, b, pad_l, pad_r):
    ch = x.shape[-1]
    y = lax.conv_general_dilated(
        x, w[:, None, :].astype(x.dtype), window_strides=(1,), padding=[(pad_l, pad_r)],
        dimension_numbers=("NWC", "WIO", "NWC"), feature_group_count=ch)
    return y + b.astype(x.dtype)


def _dwconv2d(x, w, b):
    ch = x.shape[-1]
    y = lax.conv_general_dilated(
        x, w[:, :, None, :].astype(x.dtype), window_strides=(1, 1), padding="SAME",
        dimension_numbers=("NHWC", "HWIO", "NHWC"), feature_group_count=ch)
    return y + b.astype(x.dtype)


def _block_diag(x, w, b):
    bsz, t, _ = x.shape
    nh, dh, _ = w.shape
    y = jnp.einsum("bthi,hij->bthj", x.reshape(bsz, t, nh, dh), w)
    return y.reshape(bsz, t, nh * dh) + b


def _linear_scan(a, u, h0, reverse):
    if reverse:
        a, u = jnp.flip(a, axis=1), jnp.flip(u, axis=1)
    if h0 is not None:
        u = u.at[:, 0].add(a[:, 0] * h0)

    def combine(left, right):
        a_l, u_l = left
        a_r, u_r = right
        return a_l * a_r, a_r * u_l + u_r

    _, h = lax.associative_scan(combine, (a, u), axis=1)
    return jnp.flip(h, axis=1) if reverse else h


def _rglru(xc, wa, ba, wx, bx, lam, h0, reverse):
    f32 = jnp.float32
    xf = xc.astype(f32)
    r = jax.nn.sigmoid(_block_diag(xf, wa.astype(f32), ba.astype(f32)))
    i = jax.nn.sigmoid(_block_diag(xf, wx.astype(f32), bx.astype(f32)))
    log_a = -RG_C * r * jax.nn.softplus(-lam.astype(f32))
    a = jnp.exp(log_a)
    u = jnp.sqrt(-jnp.expm1(2.0 * log_a)) * (i * xf)
    return _linear_scan(a, u, h0, reverse)


def _in_proj(h, w_in):
    return jnp.split(h @ w_in, [COL_CONV_GATE, COL_RG_X, COL_RG_GATE], axis=-1)


def _conv_group(cv, cg, conv_w, conv_b, conv_ln_g, conv_ln_b):
    u = cv * jax.nn.sigmoid(cg)
    u = _dwconv1d(u, conv_w, conv_b, CONV_PAD, CONV_PAD)
    return jax.nn.silu(_layer_norm(u, conv_ln_g, conv_ln_b))


def _mixer(h_lat, h_ctx, w_in, conv_w, conv_b, conv_ln_g, conv_ln_b, rg_conv_w, rg_conv_b,
           rg_wa, rg_ba, rg_wx, rg_bx, rg_lambda, w_out, with_ctx_out):
    cv_l, cg_l, rx_l, rg_l = _in_proj(h_lat, w_in)
    if with_ctx_out:
        cv_c, cg_c, rx_c, rg_c = _in_proj(h_ctx, w_in)
    else:
        rx_c = h_ctx @ w_in[:, COL_RG_X:COL_RG_GATE]
    xr_l = _dwconv1d(rx_l, rg_conv_w, rg_conv_b, RG_PAD_L, RG_PAD_R)
    xr_c = _dwconv1d(rx_c, rg_conv_w, rg_conv_b, RG_PAD_L, RG_PAD_R)
    dir_f = (rg_wa[0], rg_ba[0], rg_wx[0], rg_bx[0], rg_lambda[0])
    dir_b = (rg_wa[1], rg_ba[1], rg_wx[1], rg_bx[1], rg_lambda[1])
    hc_f = _rglru(xr_c, *dir_f, None, False)
    hc_b = _rglru(xr_c, *dir_b, None, True)
    hl_f = _rglru(xr_l, *dir_f, hc_f[:, -1], False)
    hl_b = _rglru(xr_l, *dir_b, hc_b[:, 0], True)
    y_rg_l = (hl_f + hl_b).astype(h_lat.dtype) * jax.nn.gelu(rg_l)
    y_cv_l = _conv_group(cv_l, cg_l, conv_w, conv_b, conv_ln_g, conv_ln_b)
    out_lat = jnp.concatenate([y_cv_l, y_rg_l], axis=-1) @ w_out
    if not with_ctx_out:
        return out_lat, None
    y_rg_c = (hc_f + hc_b).astype(h_ctx.dtype) * jax.nn.gelu(rg_c)
    y_cv_c = _conv_group(cv_c, cg_c, conv_w, conv_b, conv_ln_g, conv_ln_b)
    out_ctx = jnp.concatenate([y_cv_c, y_rg_c], axis=-1) @ w_out
    return out_lat, out_ctx


def _conv_ffn(h, rows, cols, w_up, dw_w, dw_b, w_down):
    bsz, t, _ = h.shape
    z = h @ w_up
    z = _dwconv2d(z.reshape(bsz, rows, cols, -1), dw_w, dw_b).reshape(bsz, t, -1)
    u, g = jnp.split(z, 2, axis=-1)
    return (u * jax.nn.gelu(g)) @ w_down


def setup_inputs(seed: int = 0) -> dict:
    key = jax.random.key(seed)
    ks = jax.random.split(key, 28)
    f32 = jnp.float32
    L, D, F = DEPTH, D_MODEL, FFN_DIM

    def nrm(k, shape, scale):
        return jax.random.normal(k, shape, f32) * scale

    a8 = jax.random.uniform(ks[17], (L, 2, RG_WIDTH), f32, 0.9, 0.999)
    s = a8 ** (1.0 / RG_C)
    rg_lambda = jnp.log(s) - jnp.log1p(-s)
    return {
        "x": nrm(ks[0], (BATCH, SEQ, D), 1.0),
        "c": nrm(ks[1], (BATCH, D), 1.0),
        "ctx": nrm(ks[2], (BATCH, CTX_LEN, D), 1.0),
        "c_ctx": nrm(ks[3], (D,), 1.0),
        "w_ada": nrm(ks[4], (L, D, N_MOD * D), 0.5 * D ** -0.5),
        "b_ada": nrm(ks[5], (L, N_MOD * D), 0.02),
        "w_in": nrm(ks[6], (L, D, IN_COLS), D ** -0.5),
        "conv_w": nrm(ks[7], (L, CONV_K, CONV_WIDTH), CONV_K ** -0.5),
        "conv_b": nrm(ks[8], (L, CONV_WIDTH), 0.02),
        "conv_ln_g": 1.0 + nrm(ks[9], (L, CONV_WIDTH), 0.02),
        "conv_ln_b": nrm(ks[10], (L, CONV_WIDTH), 0.02),
        "rg_conv_w": nrm(ks[11], (L, RG_CONV_K, RG_WIDTH), RG_CONV_K ** -0.5),
        "rg_conv_b": nrm(ks[12], (L, RG_WIDTH), 0.02),
        "rg_wa": nrm(ks[13], (L, 2, RG_HEADS, RG_HEAD_DIM, RG_HEAD_DIM), RG_HEAD_DIM ** -0.5),
        "rg_ba": nrm(ks[14], (L, 2, RG_WIDTH), 0.02),
        "rg_wx": nrm(ks[15], (L, 2, RG_HEADS, RG_HEAD_DIM, RG_HEAD_DIM), RG_HEAD_DIM ** -0.5),
        "rg_bx": nrm(ks[16], (L, 2, RG_WIDTH), 0.02),
        "rg_lambda": rg_lambda,
        "w_out": nrm(ks[18], (L, MIX_WIDTH, D), DEEPNORM_BETA * MIX_WIDTH ** -0.5),
        "ln1_g": 1.0 + nrm(ks[19], (L, D), 0.02),
        "ln1_b": nrm(ks[20], (L, D), 0.02),
        "ffn_up": nrm(ks[21], (L, D, 2 * F), D ** -0.5),
        "ffn_dw": nrm(ks[22], (L, FFN_CONV_K, FFN_CONV_K, 2 * F), 1.0 / FFN_CONV_K),
        "ffn_dw_b": nrm(ks[23], (L, 2 * F), 0.02),
        "ffn_down": nrm(ks[24], (L, F, D), DEEPNORM_BETA * F ** -0.5),
        "ln2_g": 1.0 + nrm(ks[25], (L, D), 0.02),
        "ln2_b": nrm(ks[26], (L, D), 0.02),
    }


def reference(x, c, ctx, c_ctx, w_ada, b_ada, w_in, conv_w, conv_b, conv_ln_g, conv_ln_b,
              rg_conv_w, rg_conv_b, rg_wa, rg_ba, rg_wx, rg_bx, rg_lambda, w_out,
              ln1_g, ln1_b, ffn_up, ffn_dw, ffn_dw_b, ffn_down, ln2_g, ln2_b):
    rows = x.shape[1] // GRID_W
    silu_c = jax.nn.silu(c)
    silu_cc = jax.nn.silu(c_ctx)[None, :]
    for l in range(DEPTH):
        keep_ctx = l < DEPTH - 1
        sh1, sc1, g1, sh2, sc2, g2 = [m[:, None, :] for m in jnp.split(silu_c @ w_ada[l] + b_ada[l], N_MOD, axis=-1)]
        csh1, csc1, cg1, csh2, csc2, cg2 = [m[:, None, :] for m in jnp.split(silu_cc @ w_ada[l] + b_ada[l], N_MOD, axis=-1)]
        mix_lat, mix_ctx = _mixer(
            _modulate(x, sh1, sc1), _modulate(ctx, csh1, csc1), w_in[l], conv_w[l], conv_b[l],
            conv_ln_g[l], conv_ln_b[l], rg_conv_w[l], rg_conv_b[l], rg_wa[l], rg_ba[l], rg_wx[l],
            rg_bx[l], rg_lambda[l], w_out[l], keep_ctx)
        x = _layer_norm(DEEPNORM_ALPHA * x + g1 * mix_lat, ln1_g[l], ln1_b[l])
        ffn_lat = _conv_ffn(_modulate(x, sh2, sc2), rows, GRID_W, ffn_up[l], ffn_dw[l], ffn_dw_b[l], ffn_down[l])
        x = _layer_norm(DEEPNORM_ALPHA * x + g2 * ffn_lat, ln2_g[l], ln2_b[l])
        if keep_ctx:
            ctx = _layer_norm(DEEPNORM_ALPHA * ctx + cg1 * mix_ctx, ln1_g[l], ln1_b[l])
            ffn_ctx = _conv_ffn(_modulate(ctx, csh2, csc2), 1, ctx.shape[1], ffn_up[l], ffn_dw[l], ffn_dw_b[l], ffn_down[l])
            ctx = _layer_norm(DEEPNORM_ALPHA * ctx + cg2 * ffn_ctx, ln2_g[l], ln2_b[l])
    return x
```

```python
import functools

import jax
import jax.numpy as jnp
from jax import lax
from jax.experimental import pallas as pl
from jax.experimental.pallas import tpu as pltpu

F32 = jnp.float32
BF16 = jnp.bfloat16

LN_EPS = 1e-5
RG_C = 8.0
N_MOD = 6
LATENT_GRID_W = 64
CONV_HALO = 16
RG_HALO = 8
VMEM_LIMIT_BYTES = 56 * 1024 * 1024


def _cparams(*sem):
    return pltpu.CompilerParams(dimension_semantics=sem, vmem_limit_bytes=VMEM_LIMIT_BYTES)


def _const_spec(shape):
    nd = len(shape)
    return pl.BlockSpec(shape, lambda *_: (0,) * nd)


def _layer_norm(v, g, b):
    mu = jnp.mean(v, axis=-1, keepdims=True)
    d = v - mu
    var = jnp.mean(d * d, axis=-1, keepdims=True)
    return d * lax.rsqrt(var + LN_EPS) * g + b


def _split_bf16(v):
    hi = v.astype(BF16)
    lo = (v - hi.astype(F32)).astype(BF16)
    return hi, lo


def _ada_kernel(c_ref, w_ref, b_ref, o_ref):
    c = c_ref[...]
    s = c * jax.nn.sigmoid(c)
    s_hi, s_lo = _split_bf16(s)
    w_hi, w_lo = _split_bf16(w_ref[0])
    acc = jnp.dot(s_hi, w_hi, preferred_element_type=F32)
    acc += jnp.dot(s_hi, w_lo, preferred_element_type=F32)
    acc += jnp.dot(s_lo, w_hi, preferred_element_type=F32)
    o_ref[0] = acc + b_ref[0]


def _ada_rows(cc, w_ada, b_ada):
    n_layers, d, nc = w_ada.shape
    rows = cc.shape[0]
    tn = 1536
    return pl.pallas_call(
        _ada_kernel,
        grid=(n_layers, nc // tn),
        in_specs=[
            pl.BlockSpec((rows, d), lambda l, j: (0, 0)),
            pl.BlockSpec((1, d, tn), lambda l, j: (l, 0, j)),
            pl.BlockSpec((1, 1, tn), lambda l, j: (l, 0, j)),
        ],
        out_specs=pl.BlockSpec((1, rows, tn), lambda l, j: (l, 0, j)),
        out_shape=jax.ShapeDtypeStruct((n_layers, rows, nc), F32),
        compiler_params=_cparams("arbitrary", "arbitrary"),
    )(cc, w_ada, b_ada.reshape(n_layers, 1, nc))


def _inproj_kernel(x_ref, mod_ref, w_ref, *out_refs, rx_only):
    x = x_ref[0]
    shift = mod_ref[0, 0:1, :]
    scale = mod_ref[0, 1:2, :]
    h = (x * (1.0 + scale) + shift).astype(BF16)
    z = jnp.dot(h, w_ref[...], preferred_element_type=F32)
    if rx_only:
        out_refs[0][0] = z
        return
    u_ref, rx_ref, gg_ref = out_refs
    cw = z.shape[1] // 4
    u_ref[0] = z[:, :cw] * jax.nn.sigmoid(z[:, cw:2 * cw])
    rx_ref[0] = z[:, 2 * cw:3 * cw]
    gg_ref[0] = jax.nn.gelu(z[:, 3 * cw:])


def _in_proj(x, mod, w_in_bf, ts, rx_only):
    bsz, t, d = x.shape
    cw = w_in_bf.shape[1] // 4
    if rx_only:
        w_in_bf = w_in_bf[:, 2 * cw:3 * cw]
    n_out = 1 if rx_only else 3
    tile = pl.BlockSpec((1, ts, cw), lambda b, i: (b, i, 0))
    outs = pl.pallas_call(
        functools.partial(_inproj_kernel, rx_only=rx_only),
        grid=(bsz, t // ts),
        in_specs=[
            pl.BlockSpec((1, ts, d), lambda b, i: (b, i, 0)),
            pl.BlockSpec((1, N_MOD, d), lambda b, i: (b, 0, 0)),
            _const_spec(w_in_bf.shape),
        ],
        out_specs=[tile] * n_out,
        out_shape=[jax.ShapeDtypeStruct((bsz, t, cw), F32)] * n_out,
        compiler_params=_cparams("arbitrary", "arbitrary"),
    )(x, mod, w_in_bf)
    return outs


def _short_conv(prev_ref, main_ref, next_ref, w_ref, b_ref, first, last):
    main = main_ref[0]
    prev = jnp.where(first, 0.0, prev_ref[0])
    nxt = jnp.where(last, 0.0, next_ref[0])
    ext = jnp.concatenate([prev, main, nxt], axis=0)
    ts = main.shape[0]
    w = w_ref[...]
    acc = b_ref[...] + w[0:1] * ext[RG_HALO - 2:RG_HALO - 2 + ts]
    for k in range(1, w.shape[0]):
        acc = acc + w[k:k + 1] * ext[RG_HALO - 2 + k:RG_HALO - 2 + k + ts]
    return acc


def _rglru_terms(xr, wax_ref, ba_ref, bx_ref, lam_ref):
    cw = xr.shape[1]
    g = jnp.dot(xr.astype(BF16), wax_ref[...], preferred_element_type=F32)
    r = jax.nn.sigmoid(g[:, :cw] + ba_ref[...])
    i = jax.nn.sigmoid(g[:, cw:] + bx_ref[...])
    nl = -lam_ref[...]
    softplus = jnp.maximum(nl, 0.0) + jnp.log1p(jnp.exp(-jnp.abs(nl)))
    log_a = -RG_C * r * softplus
    a = jnp.exp(log_a)
    th = jnp.tanh(log_a)
    u = jnp.sqrt(-2.0 * th / (1.0 - th)) * (i * xr)
    return a, u


def _tile_scan(a, u, reverse):
    t = a.shape[0]
    row = lax.broadcasted_iota(jnp.int32, a.shape, 0)
    d = 1
    while d < t:
        shift = t - d if reverse else d
        a_s = pltpu.roll(a, shift, axis=0)
        u_s = pltpu.roll(u, shift, axis=0)
        valid = (row < t - d) if reverse else (row >= d)
        u = jnp.where(valid, a * u_s + u, u)
        a = jnp.where(valid, a * a_s, a)
        d *= 2
    return a, u


def _scan_with_carry(a, u, carry, reverse):
    a_cum, h_loc = _tile_scan(a, u, reverse)
    h = h_loc + a_cum * carry
    t = h.shape[0]
    new_carry = h[0:1] if reverse else h[t - 1:t]
    return h, new_carry


def _bwd_scan_kernel(rxp_ref, rxm_ref, rxn_ref, h0_ref, cw_ref, cb_ref, wax_ref, ba_ref, bx_ref, lam_ref,
                     hb_ref, hfin_ref, carry_scr):
    step = pl.program_id(1)
    n_t = pl.num_programs(1)
    tile = n_t - 1 - step

    @pl.when(step == 0)
    def _():
        carry_scr[...] = h0_ref[0]

    xr = _short_conv(rxp_ref, rxm_ref, rxn_ref, cw_ref, cb_ref, tile == 0, tile == n_t - 1)
    a, u = _rglru_terms(xr, wax_ref, ba_ref, bx_ref, lam_ref)
    h, carry = _scan_with_carry(a, u, carry_scr[...], reverse=True)
    hb_ref[0] = h
    carry_scr[...] = carry
    hfin_ref[0] = carry


def _halo_specs(ts, halo, width, t_total, tile_of):
    per = ts // halo
    last = t_total // halo - 1
    prev = pl.BlockSpec((1, halo, width), lambda b, i: (b, jnp.maximum(tile_of(i) * per - 1, 0), 0))
    nxt = pl.BlockSpec((1, halo, width), lambda b, i: (b, jnp.minimum((tile_of(i) + 1) * per, last), 0))
    return prev, nxt


def _bwd_scan(rx, h0, rg_conv_w, rg_conv_b, wax, ba, bx, lam, ts):
    bsz, t, cw = rx.shape
    n_t = t // ts
    rev = lambda i: n_t - 1 - i
    prev, nxt = _halo_specs(ts, RG_HALO, cw, t, rev)
    row = _const_spec((1, cw))
    return pl.pallas_call(
        _bwd_scan_kernel,
        grid=(bsz, n_t),
        in_specs=[
            prev,
            pl.BlockSpec((1, ts, cw), lambda b, i: (b, rev(i), 0)),
            nxt,
            pl.BlockSpec((1, 1, cw), lambda b, i: (b, 0, 0)),
            _const_spec(rg_conv_w.shape), row, _const_spec(wax.shape), row, row, row,
        ],
        out_specs=[
            pl.BlockSpec((1, ts, cw), lambda b, i: (b, rev(i), 0)),
            pl.BlockSpec((1, 1, cw), lambda b, i: (b, 0, 0)),
        ],
        out_shape=[jax.ShapeDtypeStruct((bsz, t, cw), F32), jax.ShapeDtypeStruct((bsz, 1, cw), F32)],
        scratch_shapes=[pltpu.VMEM((1, cw), F32)],
        compiler_params=_cparams("arbitrary", "arbitrary"),
    )(rx, rx, rx, h0, rg_conv_w, rg_conv_b, wax, ba, bx, lam)


def _mix_out_kernel(x_ref, mod_ref, up_ref, um_ref, un_ref, rxp_ref, rxm_ref, rxn_ref, gg_ref, hb_ref, h0_ref,
                    cvw_ref, cvb_ref, cvg_ref, cvbeta_ref, cw_ref, cb_ref, wax_ref, ba_ref, bx_ref, lam_ref,
                    wo_ref, lng_ref, lnb_ref, o_ref, hfin_ref, carry_scr, *, alpha):
    i = pl.program_id(1)
    n_t = pl.num_programs(1)
    first = i == 0
    last = i == n_t - 1

    @pl.when(first)
    def _():
        carry_scr[...] = h0_ref[0]

    um = um_ref[0]
    ts, cw = um.shape
    ext = jnp.concatenate([jnp.where(first, 0.0, up_ref[0]), um, jnp.where(last, 0.0, un_ref[0])], axis=0)
    taps = cvw_ref[...]
    n_taps = taps.shape[0]
    off = CONV_HALO - (n_taps - 1) // 2
    acc = cvb_ref[...] + taps[0:1] * ext[off:off + ts]
    for k in range(1, n_taps):
        acc = acc + taps[k:k + 1] * ext[off + k:off + k + ts]
    v = _layer_norm(acc, cvg_ref[...], cvbeta_ref[...])
    y_cv = v * jax.nn.sigmoid(v)

    xr = _short_conv(rxp_ref, rxm_ref, rxn_ref, cw_ref, cb_ref, first, last)
    a, u = _rglru_terms(xr, wax_ref, ba_ref, bx_ref, lam_ref)
    hf, carry = _scan_with_carry(a, u, carry_scr[...], reverse=False)
    carry_scr[...] = carry
    hfin_ref[0] = carry
    y_rg = (hf + hb_ref[0]) * gg_ref[0]

    m = jnp.dot(y_cv.astype(BF16), wo_ref[0:cw, :], preferred_element_type=F32)
    m += jnp.dot(y_rg.astype(BF16), wo_ref[cw:, :], preferred_element_type=F32)
    gate = mod_ref[0, 2:3, :]
    o_ref[0] = _layer_norm(alpha * x_ref[0] + gate * m, lng_ref[...], lnb_ref[...])


def _mix_out(x, mod, u_glu, rx, gg, hb, h0, p, ts, alpha):
    bsz, t, d = x.shape
    cw = rx.shape[2]
    n_t = t // ts
    ident = lambda i: i
    up, un = _halo_specs(ts, CONV_HALO, cw, t, ident)
    rxp, rxn = _halo_specs(ts, RG_HALO, cw, t, ident)
    tile = pl.BlockSpec((1, ts, cw), lambda b, i: (b, i, 0))
    xtile = pl.BlockSpec((1, ts, d), lambda b, i: (b, i, 0))
    state = pl.BlockSpec((1, 1, cw), lambda b, i: (b, 0, 0))
    row = _const_spec((1, cw))
    drow = _const_spec((1, d))
    return pl.pallas_call(
        functools.partial(_mix_out_kernel, alpha=alpha),
        grid=(bsz, n_t),
        in_specs=[
            xtile, pl.BlockSpec((1, N_MOD, d), lambda b, i: (b, 0, 0)),
            up, tile, un, rxp, tile, rxn, tile, tile, state,
            _const_spec(p["conv_w"].shape), row, row, row,
            _const_spec(p["rg_conv_w"].shape), row, _const_spec(p["wax_f"].shape), row, row, row,
            _const_spec(p["w_out"].shape), drow, drow,
        ],
        out_specs=[xtile, state],
        out_shape=[jax.ShapeDtypeStruct((bsz, t, d), F32), jax.ShapeDtypeStruct((bsz, 1, cw), F32)],
        scratch_shapes=[pltpu.VMEM((1, cw), F32)],
        compiler_params=_cparams("arbitrary", "arbitrary"),
    )(x, mod, u_glu, u_glu, u_glu, rx, rx, rx, gg, hb, h0,
      p["conv_w"], p["conv_b"], p["conv_ln_g"], p["conv_ln_b"],
      p["rg_conv_w"], p["rg_conv_b"], p["wax_f"], p["ba_f"], p["bx_f"], p["lam_f"],
      p["w_out"], p["ln1_g"], p["ln1_b"])


def _grid_conv(z, w9, bias, width, n_out):
    m_ext = z.shape[0]
    col = lax.broadcasted_iota(jnp.int32, z.shape, 0) % width
    z_l = jnp.where(col > 0, pltpu.roll(z, 1, axis=0), 0.0)
    z_r = jnp.where(col < width - 1, pltpu.roll(z, m_ext - 1, axis=0), 0.0)
    acc = bias
    for di in range(3):
        s = di * width
        acc = acc + w9[3 * di:3 * di + 1] * z_l[s:s + n_out]
        acc = acc + w9[3 * di + 1:3 * di + 2] * z[s:s + n_out]
        acc = acc + w9[3 * di + 2:3 * di + 3] * z_r[s:s + n_out]
    return acc


def _ffn_kernel(xp_ref, xm_ref, xn_ref, mod_ref, wu_ref, wg_ref, dwu_ref, dwg_ref, bu_ref, bg_ref, wd_ref,
                lng_ref, lnb_ref, o_ref, h_scr, acc_scr, *, width, alpha):
    r = pl.program_id(1)
    f = pl.program_id(2)
    n_r = pl.num_programs(1)
    n_f = pl.num_programs(2)
    n_out = xm_ref.shape[1]

    @pl.when(f == 0)
    def _():
        shift = mod_ref[0, 3:4, :]
        scale = mod_ref[0, 4:5, :]
        h_scr[0:width] = (xp_ref[0] * (1.0 + scale) + shift).astype(BF16)
        h_scr[width:width + n_out] = (xm_ref[0] * (1.0 + scale) + shift).astype(BF16)
        h_scr[width + n_out:] = (xn_ref[0] * (1.0 + scale) + shift).astype(BF16)
        acc_scr[...] = jnp.zeros_like(acc_scr)

    h = h_scr[...]
    m_ext = h.shape[0]
    row = lax.broadcasted_iota(jnp.int32, (m_ext, 1), 0)
    keep = jnp.logical_and(jnp.logical_or(row >= width, r > 0),
                           jnp.logical_or(row < width + n_out, r < n_r - 1))
    zu = jnp.where(keep, jnp.dot(h, wu_ref[...], preferred_element_type=F32), 0.0)
    zg = jnp.where(keep, jnp.dot(h, wg_ref[...], preferred_element_type=F32), 0.0)
    cu = _grid_conv(zu, dwu_ref[...], bu_ref[...], width, n_out)
    cg = _grid_conv(zg, dwg_ref[...], bg_ref[...], width, n_out)
    y = (cu * jax.nn.gelu(cg)).astype(BF16)
    acc_scr[...] += jnp.dot(y, wd_ref[...], preferred_element_type=F32)

    @pl.when(f == n_f - 1)
    def _():
        gate = mod_ref[0, 5:6, :]
        o_ref[0] = _layer_norm(alpha * xm_ref[0] + gate * acc_scr[...], lng_ref[...], lnb_ref[...])


def _conv_ffn(x, mod, p, width, rows_per_tile, fc, alpha):
    bsz, t, d = x.shape
    n_out = rows_per_tile * width
    n_r = t // n_out
    f_dim = p["ffn_down"].shape[0]
    n_f = f_dim // fc
    last_row = t // width - 1
    xp = pl.BlockSpec((1, width, d), lambda b, r, f: (b, jnp.maximum(r * rows_per_tile - 1, 0), 0))
    xm = pl.BlockSpec((1, n_out, d), lambda b, r, f: (b, r, 0))
    xn = pl.BlockSpec((1, width, d), lambda b, r, f: (b, jnp.minimum((r + 1) * rows_per_tile, last_row), 0))
    drow = _const_spec((1, d))
    return pl.pallas_call(
        functools.partial(_ffn_kernel, width=width, alpha=alpha),
        grid=(bsz, n_r, n_f),
        in_specs=[
            xp, xm, xn, pl.BlockSpec((1, N_MOD, d), lambda b, r, f: (b, 0, 0)),
            pl.BlockSpec((d, fc), lambda b, r, f: (0, f)),
            pl.BlockSpec((d, fc), lambda b, r, f: (0, n_f + f)),
            pl.BlockSpec((9, fc), lambda b, r, f: (0, f)),
            pl.BlockSpec((9, fc), lambda b, r, f: (0, n_f + f)),
            pl.BlockSpec((1, fc), lambda b, r, f: (0, f)),
            pl.BlockSpec((1, fc), lambda b, r, f: (0, n_f + f)),
            pl.BlockSpec((fc, d), lambda b, r, f: (f, 0)),
            drow, drow,
        ],
        out_specs=pl.BlockSpec((1, n_out, d), lambda b, r, f: (b, r, 0)),
        out_shape=jax.ShapeDtypeStruct((bsz, t, d), F32),
        scratch_shapes=[pltpu.VMEM((n_out + 2 * width, d), BF16), pltpu.VMEM((n_out, d), F32)],
        compiler_params=_cparams("arbitrary", "arbitrary", "arbitrary"),
    )(x, x, x, mod, p["ffn_up"], p["ffn_up"], p["ffn_dw"], p["ffn_dw"], p["ffn_dw_b"], p["ffn_dw_b"],
      p["ffn_down"], p["ln2_g"], p["ln2_b"])


def _seq_tile(t):
    return 512 if t % 512 == 0 else t


def _ffn_rows_per_tile(rows):
    return 16 if rows % 16 == 0 else rows


def _ffn_chunk(f_dim):
    return 256 if f_dim % 256 == 0 else 128


def _block_diag_dense(w):
    nh, dh, _ = w.shape
    eye = jnp.eye(nh, dtype=w.dtype)
    return jnp.einsum("hij,hg->higj", w, eye).reshape(nh * dh, nh * dh)


def _layer_params(l, w_in, conv_w, conv_b, conv_ln_g, conv_ln_b, rg_conv_w, rg_conv_b, rg_wa, rg_ba, rg_wx,
                  rg_bx, rg_lambda, w_out, ln1_g, ln1_b, ffn_up, ffn_dw, ffn_dw_b, ffn_down, ln2_g, ln2_b):
    row = lambda v: v.reshape(1, -1)
    p = {
        "w_in": w_in[l].astype(BF16),
        "conv_w": conv_w[l], "conv_b": row(conv_b[l]),
        "conv_ln_g": row(conv_ln_g[l]), "conv_ln_b": row(conv_ln_b[l]),
        "rg_conv_w": rg_conv_w[l], "rg_conv_b": row(rg_conv_b[l]),
        "w_out": w_out[l].astype(BF16), "ln1_g": row(ln1_g[l]), "ln1_b": row(ln1_b[l]),
        "ffn_up": ffn_up[l].astype(BF16), "ffn_dw": ffn_dw[l].reshape(9, -1), "ffn_dw_b": row(ffn_dw_b[l]),
        "ffn_down": ffn_down[l].astype(BF16), "ln2_g": row(ln2_g[l]), "ln2_b": row(ln2_b[l]),
    }
    for k, name in ((0, "f"), (1, "b")):
        p["wax_" + name] = jnp.concatenate(
            [_block_diag_dense(rg_wa[l, k]), _block_diag_dense(rg_wx[l, k])], axis=1).astype(BF16)
        p["ba_" + name] = row(rg_ba[l, k])
        p["bx_" + name] = row(rg_bx[l, k])
        p["lam_" + name] = row(rg_lambda[l, k])
    return p


def _mixer(x, mod, p, h0_f, h0_b, alpha):
    ts = _seq_tile(x.shape[1])
    u_glu, rx, gg = _in_proj(x, mod, p["w_in"], ts, False)
    hb, hb_fin = _bwd_scan(rx, h0_b, p["rg_conv_w"], p["rg_conv_b"], p["wax_b"], p["ba_b"], p["bx_b"],
                           p["lam_b"], ts)
    x_new, hf_fin = _mix_out(x, mod, u_glu, rx, gg, hb, h0_f, p, ts, alpha)
    return x_new, hf_fin, hb_fin


def kernel(x, c, ctx, c_ctx, w_ada, b_ada, w_in, conv_w, conv_b, conv_ln_g, conv_ln_b, rg_conv_w, rg_conv_b,
           rg_wa, rg_ba, rg_wx, rg_bx, rg_lambda, w_out, ln1_g, ln1_b, ffn_up, ffn_dw, ffn_dw_b, ffn_down,
           ln2_g, ln2_b):
    bsz, seq, d = x.shape
    ctx_len = ctx.shape[1]
    depth = w_in.shape[0]
    alpha = (2.0 * depth) ** 0.25
    cw = rg_lambda.shape[-1]

    pad = (-(bsz + 1)) % 8
    cc = jnp.concatenate([c, c_ctx[None, :], jnp.zeros((pad, d), F32)], axis=0)
    mods = _ada_rows(cc, w_ada, b_ada).reshape(depth, bsz + 1 + pad, N_MOD, d)
    zeros_state = jnp.zeros((bsz, 1, cw), F32)

    for l in range(depth):
        p = _layer_params(l, w_in, conv_w, conv_b, conv_ln_g, conv_ln_b, rg_conv_w, rg_conv_b, rg_wa, rg_ba,
                          rg_wx, rg_bx, rg_lambda, w_out, ln1_g, ln1_b, ffn_up, ffn_dw, ffn_dw_b, ffn_down,
                          ln2_g, ln2_b)
        mod_lat = mods[l, :bsz]
        mod_ctx = jnp.broadcast_to(mods[l, bsz:bsz + 1], (bsz, N_MOD, d))
        ctx_mid, hc_f, hc_b = _mixer(ctx, mod_ctx, p, zeros_state, zeros_state, alpha)
        x, _, _ = _mixer(x, mod_lat, p, hc_f, hc_b, alpha)
        rows = seq // LATENT_GRID_W
        x = _conv_ffn(x, mod_lat, p, LATENT_GRID_W, _ffn_rows_per_tile(rows), _ffn_chunk(ffn_down.shape[1]), alpha)
        if l < depth - 1:
            ctx = _conv_ffn(ctx_mid, mod_ctx, p, ctx_len, 1, _ffn_chunk(ffn_down.shape[1]), alpha)
    return x
```

```python
import functools

import jax
import jax.numpy as jnp
from jax import lax
from jax.experimental import pallas as pl
from jax.experimental.pallas import tpu as pltpu

F32 = jnp.float32
BF16 = jnp.bfloat16

LN_EPS = 1e-5
RG_C = 8.0
N_MOD = 6
LATENT_GRID_W = 64
CONV_HALO = 16
RG_HALO = 8
VMEM_LIMIT_BYTES = 56 * 1024 * 1024
LANES = 128


def _cparams(*sem):
    return pltpu.CompilerParams(dimension_semantics=sem, vmem_limit_bytes=VMEM_LIMIT_BYTES)


def _const_spec(shape):
    nd = len(shape)
    return pl.BlockSpec(shape, lambda *_: (0,) * nd)


def _layer_norm(v, g, b):
    mu = jnp.mean(v, axis=-1, keepdims=True)
    d = v - mu
    var = jnp.mean(d * d, axis=-1, keepdims=True)
    return d * lax.rsqrt(var + LN_EPS) * g + b


def _split_bf16(v):
    hi = v.astype(BF16)
    lo = (v - hi.astype(F32)).astype(BF16)
    return hi, lo


def _ada_kernel(c_ref, w_ref, b_ref, o_ref):
    c = c_ref[...]
    s = c * jax.nn.sigmoid(c)
    s_hi, s_lo = _split_bf16(s)
    w_hi, w_lo = _split_bf16(w_ref[0])
    acc = jnp.dot(s_hi, w_hi, preferred_element_type=F32)
    acc += jnp.dot(s_hi, w_lo, preferred_element_type=F32)
    acc += jnp.dot(s_lo, w_hi, preferred_element_type=F32)
    o_ref[0] = acc + b_ref[0]


def _ada_rows(cc, w_ada, b_ada):
    n_layers, d, nc = w_ada.shape
    rows = cc.shape[0]
    tn = 1536
    return pl.pallas_call(
        _ada_kernel,
        name="ada_rows",
        grid=(n_layers, nc // tn),
        in_specs=[
            pl.BlockSpec((rows, d), lambda l, j: (0, 0)),
            pl.BlockSpec((1, d, tn), lambda l, j: (l, 0, j)),
            pl.BlockSpec((1, 1, tn), lambda l, j: (l, 0, j)),
        ],
        out_specs=pl.BlockSpec((1, rows, tn), lambda l, j: (l, 0, j)),
        out_shape=jax.ShapeDtypeStruct((n_layers, rows, nc), F32),
        compiler_params=_cparams("arbitrary", "arbitrary"),
    )(cc, w_ada, b_ada.reshape(n_layers, 1, nc))


def _inproj_kernel(x_ref, mod_ref, w_ref, *out_refs, rx_only):
    x = x_ref[0]
    shift = mod_ref[0, 0:1, :]
    scale = mod_ref[0, 1:2, :]
    h = (x * (1.0 + scale) + shift).astype(BF16)
    z = jnp.dot(h, w_ref[...], preferred_element_type=F32)
    if rx_only:
        out_refs[0][0] = z
        return
    u_ref, rx_ref, gg_ref = out_refs
    cw = z.shape[1] // 4
    u_ref[0] = z[:, :cw] * jax.nn.sigmoid(z[:, cw:2 * cw])
    rx_ref[0] = z[:, 2 * cw:3 * cw]
    gg_ref[0] = jax.nn.gelu(z[:, 3 * cw:])


def _in_proj(x, mod, w_in_bf, ts, rx_only):
    bsz, t, d = x.shape
    cw = w_in_bf.shape[1] // 4
    if rx_only:
        w_in_bf = w_in_bf[:, 2 * cw:3 * cw]
    n_out = 1 if rx_only else 3
    tile = pl.BlockSpec((1, ts, cw), lambda b, i: (b, i, 0))
    outs = pl.pallas_call(
        functools.partial(_inproj_kernel, rx_only=rx_only),
        name="in_proj",
        grid=(bsz, t // ts),
        in_specs=[
            pl.BlockSpec((1, ts, d), lambda b, i: (b, i, 0)),
            pl.BlockSpec((1, N_MOD, d), lambda b, i: (b, 0, 0)),
            _const_spec(w_in_bf.shape),
        ],
        out_specs=[tile] * n_out,
        out_shape=[jax.ShapeDtypeStruct((bsz, t, cw), F32)] * n_out,
        compiler_params=_cparams("arbitrary", "arbitrary"),
    )(x, mod, w_in_bf)
    return outs


def _short_conv(prev_ref, main_ref, next_ref, w_ref, b_ref, first, last):
    main = main_ref[0]
    prev = jnp.where(first, 0.0, prev_ref[0])
    nxt = jnp.where(last, 0.0, next_ref[0])
    ext = jnp.concatenate([prev, main, nxt], axis=0)
    ts = main.shape[0]
    w = w_ref[...]
    acc = b_ref[...] + w[0:1] * ext[RG_HALO - 2:RG_HALO - 2 + ts]
    for k in range(1, w.shape[0]):
        acc = acc + w[k:k + 1] * ext[RG_HALO - 2 + k:RG_HALO - 2 + k + ts]
    return acc


def _rglru_terms(xr, wax_ref, ba_ref, bx_ref, lam_ref):
    cw = xr.shape[1]
    g = jnp.dot(xr.astype(BF16), wax_ref[...], preferred_element_type=F32)
    r = jax.nn.sigmoid(g[:, :cw] + ba_ref[...])
    i = jax.nn.sigmoid(g[:, cw:] + bx_ref[...])
    nl = -lam_ref[...]
    softplus = jnp.maximum(nl, 0.0) + jnp.log1p(jnp.exp(-jnp.abs(nl)))
    log_a = -RG_C * r * softplus
    a = jnp.exp(log_a)
    th = jnp.tanh(log_a)
    u = jnp.sqrt(-2.0 * th / (1.0 - th)) * (i * xr)
    return a, u


def _tile_scan(a, u, reverse):
    t = a.shape[0]
    row = lax.broadcasted_iota(jnp.int32, a.shape, 0)
    d = 1
    while d < t:
        shift = t - d if reverse else d
        a_s = pltpu.roll(a, shift, axis=0)
        u_s = pltpu.roll(u, shift, axis=0)
        valid = (row < t - d) if reverse else (row >= d)
        u = jnp.where(valid, a * u_s + u, u)
        a = jnp.where(valid, a * a_s, a)
        d *= 2
    return a, u


def _scan_with_carry(a, u, carry, reverse):
    a_cum, h_loc = _tile_scan(a, u, reverse)
    h = h_loc + a_cum * carry
    t = h.shape[0]
    new_carry = h[0:1] if reverse else h[t - 1:t]
    return h, new_carry


def _bwd_scan_kernel(rxp_ref, rxm_ref, rxn_ref, h0_ref, cw_ref, cb_ref, wax_ref, ba_ref, bx_ref, lam_ref,
                     hb_ref, hfin_ref, carry_scr):
    step = pl.program_id(1)
    n_t = pl.num_programs(1)
    tile = n_t - 1 - step

    @pl.when(step == 0)
    def _():
        carry_scr[...] = h0_ref[0]

    xr = _short_conv(rxp_ref, rxm_ref, rxn_ref, cw_ref, cb_ref, tile == 0, tile == n_t - 1)
    a, u = _rglru_terms(xr, wax_ref, ba_ref, bx_ref, lam_ref)
    h, carry = _scan_with_carry(a, u, carry_scr[...], reverse=True)
    hb_ref[0] = h
    carry_scr[...] = carry
    hfin_ref[0] = carry


def _halo_specs(ts, halo, width, t_total, tile_of):
    per = ts // halo
    last = t_total // halo - 1
    prev = pl.BlockSpec((1, halo, width), lambda b, i: (b, jnp.maximum(tile_of(i) * per - 1, 0), 0))
    nxt = pl.BlockSpec((1, halo, width), lambda b, i: (b, jnp.minimum((tile_of(i) + 1) * per, last), 0))
    return prev, nxt


def _bwd_scan(rx, h0, rg_conv_w, rg_conv_b, wax, ba, bx, lam, ts):
    bsz, t, cw = rx.shape
    n_t = t // ts
    rev = lambda i: n_t - 1 - i
    prev, nxt = _halo_specs(ts, RG_HALO, cw, t, rev)
    row = _const_spec((1, cw))
    return pl.pallas_call(
        _bwd_scan_kernel,
        name="bwd_scan",
        grid=(bsz, n_t),
        in_specs=[
            prev,
            pl.BlockSpec((1, ts, cw), lambda b, i: (b, rev(i), 0)),
            nxt,
            pl.BlockSpec((1, 1, cw), lambda b, i: (b, 0, 0)),
            _const_spec(rg_conv_w.shape), row, _const_spec(wax.shape), row, row, row,
        ],
        out_specs=[
            pl.BlockSpec((1, ts, cw), lambda b, i: (b, rev(i), 0)),
            pl.BlockSpec((1, 1, cw), lambda b, i: (b, 0, 0)),
        ],
        out_shape=[jax.ShapeDtypeStruct((bsz, t, cw), F32), jax.ShapeDtypeStruct((bsz, 1, cw), F32)],
        scratch_shapes=[pltpu.VMEM((1, cw), F32)],
        compiler_params=_cparams("arbitrary", "arbitrary"),
    )(rx, rx, rx, h0, rg_conv_w, rg_conv_b, wax, ba, bx, lam)


def _mix_out_kernel(x_ref, mod_ref, up_ref, um_ref, un_ref, rxp_ref, rxm_ref, rxn_ref, gg_ref, hb_ref, h0_ref,
                    cvw_ref, cvb_ref, cvg_ref, cvbeta_ref, cw_ref, cb_ref, wax_ref, ba_ref, bx_ref, lam_ref,
                    wo_ref, lng_ref, lnb_ref, o_ref, hfin_ref, carry_scr, *, alpha):
    i = pl.program_id(1)
    n_t = pl.num_programs(1)
    first = i == 0
    last = i == n_t - 1

    @pl.when(first)
    def _():
        carry_scr[...] = h0_ref[0]

    um = um_ref[0]
    ts, cw = um.shape
    ext = jnp.concatenate([jnp.where(first, 0.0, up_ref[0]), um, jnp.where(last, 0.0, un_ref[0])], axis=0)
    taps = cvw_ref[...]
    n_taps = taps.shape[0]
    off = CONV_HALO - (n_taps - 1) // 2
    acc = cvb_ref[...] + taps[0:1] * ext[off:off + ts]
    for k in range(1, n_taps):
        acc = acc + taps[k:k + 1] * ext[off + k:off + k + ts]
    v = _layer_norm(acc, cvg_ref[...], cvbeta_ref[...])
    y_cv = v * jax.nn.sigmoid(v)

    xr = _short_conv(rxp_ref, rxm_ref, rxn_ref, cw_ref, cb_ref, first, last)
    a, u = _rglru_terms(xr, wax_ref, ba_ref, bx_ref, lam_ref)
    hf, carry = _scan_with_carry(a, u, carry_scr[...], reverse=False)
    carry_scr[...] = carry
    hfin_ref[0] = carry
    y_rg = (hf + hb_ref[0]) * gg_ref[0]

    m = jnp.dot(y_cv.astype(BF16), wo_ref[0:cw, :], preferred_element_type=F32)
    m += jnp.dot(y_rg.astype(BF16), wo_ref[cw:, :], preferred_element_type=F32)
    gate = mod_ref[0, 2:3, :]
    o_ref[0] = _layer_norm(alpha * x_ref[0] + gate * m, lng_ref[...], lnb_ref[...])


def _mix_out(x, mod, u_glu, rx, gg, hb, h0, p, ts, alpha):
    bsz, t, d = x.shape
    cw = rx.shape[2]
    n_t = t // ts
    ident = lambda i: i
    up, un = _halo_specs(ts, CONV_HALO, cw, t, ident)
    rxp, rxn = _halo_specs(ts, RG_HALO, cw, t, ident)
    tile = pl.BlockSpec((1, ts, cw), lambda b, i: (b, i, 0))
    xtile = pl.BlockSpec((1, ts, d), lambda b, i: (b, i, 0))
    state = pl.BlockSpec((1, 1, cw), lambda b, i: (b, 0, 0))
    row = _const_spec((1, cw))
    drow = _const_spec((1, d))
    return pl.pallas_call(
        functools.partial(_mix_out_kernel, alpha=alpha),
        name="mix_out",
        grid=(bsz, n_t),
        in_specs=[
            xtile, pl.BlockSpec((1, N_MOD, d), lambda b, i: (b, 0, 0)),
            up, tile, un, rxp, tile, rxn, tile, tile, state,
            _const_spec(p["conv_w"].shape), row, row, row,
            _const_spec(p["rg_conv_w"].shape), row, _const_spec(p["wax_f"].shape), row, row, row,
            _const_spec(p["w_out"].shape), drow, drow,
        ],
        out_specs=[xtile, state],
        out_shape=[jax.ShapeDtypeStruct((bsz, t, d), F32), jax.ShapeDtypeStruct((bsz, 1, cw), F32)],
        scratch_shapes=[pltpu.VMEM((1, cw), F32)],
        compiler_params=_cparams("arbitrary", "arbitrary"),
    )(x, mod, u_glu, u_glu, u_glu, rx, rx, rx, gg, hb, h0,
      p["conv_w"], p["conv_b"], p["conv_ln_g"], p["conv_ln_b"],
      p["rg_conv_w"], p["rg_conv_b"], p["wax_f"], p["ba_f"], p["bx_f"], p["lam_f"],
      p["w_out"], p["ln1_g"], p["ln1_b"])


def _grid_conv_row(z_ref, slot, col0, row0, width, w9, bias, multi_row):
    lanes = pl.ds(col0, LANES)
    if multi_row:
        zt = z_ref[slot, pl.ds(row0, width), lanes]
        zm = z_ref[slot, pl.ds(row0 + width, width), lanes]
        zb = z_ref[slot, pl.ds(row0 + 2 * width, width), lanes]
        v = [w9[dj] * zt + w9[3 + dj] * zm + w9[6 + dj] * zb for dj in range(3)]
    else:
        zm = z_ref[slot, pl.ds(row0, width), lanes]
        v = [w9[3 + dj] * zm for dj in range(3)]
    pos = lax.broadcasted_iota(jnp.int32, (width, LANES), 0)
    left = jnp.where(pos == 0, 0.0, pltpu.roll(v[0], 1, axis=0))
    right = jnp.where(pos == width - 1, 0.0, pltpu.roll(v[2], width - 1, axis=0))
    return (v[1] + bias) + (left + right)


def _ffn_kernel(*refs, width, multi_row, alpha):
    if multi_row:
        xp_ref, xm_ref, xn_ref = refs[:3]
        refs = refs[3:]
    else:
        xm_ref = refs[0]
        refs = refs[1:]
    mod_ref, wu_ref, dw_ref, db_ref, wd_ref, lng_ref, lnb_ref, o_ref, h_scr, z_scr, y_scr, acc_scr = refs
    n_out = xm_ref.shape[1]
    n_f, fc, _ = wd_ref.shape
    off = width if multi_row else 0

    shift = mod_ref[0, 3:4, :]
    scale = mod_ref[0, 4:5, :]
    h_scr[off:off + n_out] = (xm_ref[0] * (1.0 + scale) + shift).astype(BF16)
    if multi_row:
        r = pl.program_id(1)
        n_r = pl.num_programs(1)
        hp = xp_ref[0] * (1.0 + scale) + shift
        hn = xn_ref[0] * (1.0 + scale) + shift
        h_scr[0:width] = jnp.where(r == 0, 0.0, hp).astype(BF16)
        h_scr[off + n_out:] = jnp.where(r == n_r - 1, 0.0, hn).astype(BF16)
    acc_scr[...] = jnp.zeros_like(acc_scr)

    def up_project(f, slot):
        h = h_scr[...]
        z_scr[slot, :, 0:fc] = jnp.dot(h, wu_ref[f], preferred_element_type=F32)
        z_scr[slot, :, fc:2 * fc] = jnp.dot(h, wu_ref[n_f + f], preferred_element_type=F32)

    def conv_gate_down(f, slot):
        for cb in range(0, fc, LANES):
            wu9 = [dw_ref[f, k:k + 1, cb:cb + LANES] for k in range(9)]
            wg9 = [dw_ref[n_f + f, k:k + 1, cb:cb + LANES] for k in range(9)]
            bu = db_ref[f, :, cb:cb + LANES]
            bg = db_ref[n_f + f, :, cb:cb + LANES]
            for row0 in range(0, n_out, width):
                cu = _grid_conv_row(z_scr, slot, cb, row0, width, wu9, bu, multi_row)
                cg = _grid_conv_row(z_scr, slot, fc + cb, row0, width, wg9, bg, multi_row)
                y_scr[pl.ds(row0, width), pl.ds(cb, LANES)] = (cu * jax.nn.gelu(cg)).astype(BF16)
        acc_scr[...] += jnp.dot(y_scr[...], wd_ref[f], preferred_element_type=F32)

    up_project(0, 0)

    def step(f, carry):
        up_project(f + 1, (f + 1) % 2)
        conv_gate_down(f, f % 2)
        return carry

    lax.fori_loop(0, n_f - 1, step, 0)
    conv_gate_down(n_f - 1, (n_f - 1) % 2)

    gate = mod_ref[0, 5:6, :]
    o_ref[0] = _layer_norm(alpha * xm_ref[0] + gate * acc_scr[...], lng_ref[...], lnb_ref[...])


def _resident_spec(shape):
    nd = len(shape)
    return pl.BlockSpec(shape, lambda *_: (0,) * nd, pipeline_mode=pl.Buffered(1))


def _conv_ffn(x, mod, p, width, rows_per_tile, alpha):
    bsz, t, d = x.shape
    n_out = rows_per_tile * width
    n_r = t // n_out
    multi_row = t > width
    n_f, fc, _ = p["ffn_down"].shape
    last_row = t // width - 1
    x_specs = [pl.BlockSpec((1, n_out, d), lambda b, r: (b, r, 0))]
    x_args = [x]
    if multi_row:
        x_specs = [
            pl.BlockSpec((1, width, d), lambda b, r: (b, jnp.maximum(r * rows_per_tile - 1, 0), 0)),
            x_specs[0],
            pl.BlockSpec((1, width, d), lambda b, r: (b, jnp.minimum((r + 1) * rows_per_tile, last_row), 0)),
        ]
        x_args = [x, x, x]
    m_ext = n_out + (2 * width if multi_row else 0)
    drow = _const_spec((1, d))
    return pl.pallas_call(
        functools.partial(_ffn_kernel, width=width, multi_row=multi_row, alpha=alpha),
        name="conv_ffn",
        grid=(bsz, n_r),
        in_specs=x_specs + [
            pl.BlockSpec((1, N_MOD, d), lambda b, r: (b, 0, 0)),
            _resident_spec(p["ffn_up"].shape), _resident_spec(p["ffn_dw"].shape),
            _resident_spec(p["ffn_dw_b"].shape), _resident_spec(p["ffn_down"].shape),
            drow, drow,
        ],
        out_specs=pl.BlockSpec((1, n_out, d), lambda b, r: (b, r, 0)),
        out_shape=jax.ShapeDtypeStruct((bsz, t, d), F32),
        scratch_shapes=[
            pltpu.VMEM((m_ext, d), BF16),
            pltpu.VMEM((2, m_ext, 2 * fc), F32),
            pltpu.VMEM((n_out, fc), BF16),
            pltpu.VMEM((n_out, d), F32),
        ],
        compiler_params=_cparams("arbitrary", "arbitrary"),
    )(*x_args, mod, p["ffn_up"], p["ffn_dw"], p["ffn_dw_b"], p["ffn_down"], p["ln2_g"], p["ln2_b"])


def _seq_tile(t):
    return 512 if t % 512 == 0 else t


def _ffn_rows_per_tile(rows):
    return 16 if rows % 16 == 0 else rows


def _ffn_chunk(f_dim):
    return 256 if f_dim % 256 == 0 else 128


def _block_diag_dense(w):
    nh, dh, _ = w.shape
    eye = jnp.eye(nh, dtype=w.dtype)
    return jnp.einsum("hij,hg->higj", w, eye).reshape(nh * dh, nh * dh)


def _layer_params(l, w_in, conv_w, conv_b, conv_ln_g, conv_ln_b, rg_conv_w, rg_conv_b, rg_wa, rg_ba, rg_wx,
                  rg_bx, rg_lambda, w_out, ln1_g, ln1_b, ffn_up, ffn_dw, ffn_dw_b, ffn_down, ln2_g, ln2_b):
    row = lambda v: v.reshape(1, -1)
    p = {
        "w_in": w_in[l].astype(BF16),
        "conv_w": conv_w[l], "conv_b": row(conv_b[l]),
        "conv_ln_g": row(conv_ln_g[l]), "conv_ln_b": row(conv_ln_b[l]),
        "rg_conv_w": rg_conv_w[l], "rg_conv_b": row(rg_conv_b[l]),
        "w_out": w_out[l].astype(BF16), "ln1_g": row(ln1_g[l]), "ln1_b": row(ln1_b[l]),
        "ln2_g": row(ln2_g[l]), "ln2_b": row(ln2_b[l]),
    }
    f_dim, d = ffn_down.shape[1:]
    fc = _ffn_chunk(f_dim)
    n_c = 2 * f_dim // fc
    p["ffn_up"] = ffn_up[l].astype(BF16).reshape(d, n_c, fc).transpose(1, 0, 2)
    p["ffn_dw"] = ffn_dw[l].reshape(9, n_c, fc).transpose(1, 0, 2)
    p["ffn_dw_b"] = ffn_dw_b[l].reshape(n_c, 1, fc)
    p["ffn_down"] = ffn_down[l].astype(BF16).reshape(f_dim // fc, fc, d)
    for k, name in ((0, "f"), (1, "b")):
        p["wax_" + name] = jnp.concatenate(
            [_block_diag_dense(rg_wa[l, k]), _block_diag_dense(rg_wx[l, k])], axis=1).astype(BF16)
        p["ba_" + name] = row(rg_ba[l, k])
        p["bx_" + name] = row(rg_bx[l, k])
        p["lam_" + name] = row(rg_lambda[l, k])
    return p


def _mixer(x, mod, p, h0_f, h0_b, alpha):
    ts = _seq_tile(x.shape[1])
    u_glu, rx, gg = _in_proj(x, mod, p["w_in"], ts, False)
    hb, hb_fin = _bwd_scan(rx, h0_b, p["rg_conv_w"], p["rg_conv_b"], p["wax_b"], p["ba_b"], p["bx_b"],
                           p["lam_b"], ts)
    x_new, hf_fin = _mix_out(x, mod, u_glu, rx, gg, hb, h0_f, p, ts, alpha)
    return x_new, hf_fin, hb_fin


def kernel(x, c, ctx, c_ctx, w_ada, b_ada, w_in, conv_w, conv_b, conv_ln_g, conv_ln_b, rg_conv_w, rg_conv_b,
           rg_wa, rg_ba, rg_wx, rg_bx, rg_lambda, w_out, ln1_g, ln1_b, ffn_up, ffn_dw, ffn_dw_b, ffn_down,
           ln2_g, ln2_b):
    bsz, seq, d = x.shape
    ctx_len = ctx.shape[1]
    depth = w_in.shape[0]
    alpha = (2.0 * depth) ** 0.25
    cw = rg_lambda.shape[-1]

    pad = (-(bsz + 1)) % 8
    cc = jnp.concatenate([c, c_ctx[None, :], jnp.zeros((pad, d), F32)], axis=0)
    mods = _ada_rows(cc, w_ada, b_ada).reshape(depth, bsz + 1 + pad, N_MOD, d)
    zeros_state = jnp.zeros((bsz, 1, cw), F32)

    for l in range(depth):
        p = _layer_params(l, w_in, conv_w, conv_b, conv_ln_g, conv_ln_b, rg_conv_w, rg_conv_b, rg_wa, rg_ba,
                          rg_wx, rg_bx, rg_lambda, w_out, ln1_g, ln1_b, ffn_up, ffn_dw, ffn_dw_b, ffn_down,
                          ln2_g, ln2_b)
        mod_lat = mods[l, :bsz]
        mod_ctx = jnp.broadcast_to(mods[l, bsz:bsz + 1], (bsz, N_MOD, d))
        ctx_mid, hc_f, hc_b = _mixer(ctx, mod_ctx, p, zeros_state, zeros_state, alpha)
        x, _, _ = _mixer(x, mod_lat, p, hc_f, hc_b, alpha)
        rows = seq // LATENT_GRID_W
        x = _conv_ffn(x, mod_lat, p, LATENT_GRID_W, _ffn_rows_per_tile(rows), alpha)
        if l < depth - 1:
            ctx = _conv_ffn(ctx_mid, mod_ctx, p, ctx_len, 1, alpha)
    return x
```

```python
import functools

import jax
import jax.numpy as jnp
from jax import lax
from jax.experimental import pallas as pl
from jax.experimental.pallas import tpu as pltpu

F32 = jnp.float32
BF16 = jnp.bfloat16

LN_EPS = 1e-5
RG_C = 8.0
N_MOD = 6
LATENT_GRID_W = 64
CONV_HALO = 16
VMEM_LIMIT_BYTES = 56 * 1024 * 1024
LANES = 128
SUBLANES = 8


def _cparams(*sem):
    return pltpu.CompilerParams(dimension_semantics=sem, vmem_limit_bytes=VMEM_LIMIT_BYTES)


def _const_spec(shape):
    nd = len(shape)
    return pl.BlockSpec(shape, lambda *_: (0,) * nd)


def _resident_spec(shape):
    nd = len(shape)
    return pl.BlockSpec(shape, lambda *_: (0,) * nd, pipeline_mode=pl.Buffered(1))


def _layer_norm(v, g, b):
    mu = jnp.mean(v, axis=-1, keepdims=True)
    d = v - mu
    var = jnp.mean(d * d, axis=-1, keepdims=True)
    return d * lax.rsqrt(var + LN_EPS) * g + b


def _split_bf16(v):
    hi = v.astype(BF16)
    lo = (v - hi.astype(F32)).astype(BF16)
    return hi, lo


def _ada_kernel(c_ref, w_ref, b_ref, o_ref):
    c = c_ref[...]
    s = c * jax.nn.sigmoid(c)
    s_hi, s_lo = _split_bf16(s)
    w_hi, w_lo = _split_bf16(w_ref[0])
    acc = jnp.dot(s_hi, w_hi, preferred_element_type=F32)
    acc += jnp.dot(s_hi, w_lo, preferred_element_type=F32)
    acc += jnp.dot(s_lo, w_hi, preferred_element_type=F32)
    o_ref[0] = acc + b_ref[0]


def _ada_rows(cc, w_ada, b_ada):
    n_layers, d, nc = w_ada.shape
    rows = cc.shape[0]
    tn = 1536
    return pl.pallas_call(
        _ada_kernel,
        name="ada_rows",
        grid=(n_layers, nc // tn),
        in_specs=[
            pl.BlockSpec((rows, d), lambda l, j: (0, 0)),
            pl.BlockSpec((1, d, tn), lambda l, j: (l, 0, j)),
            pl.BlockSpec((1, 1, tn), lambda l, j: (l, 0, j)),
        ],
        out_specs=pl.BlockSpec((1, rows, tn), lambda l, j: (l, 0, j)),
        out_shape=jax.ShapeDtypeStruct((n_layers, rows, nc), F32),
        compiler_params=_cparams("arbitrary", "arbitrary"),
    )(cc, w_ada, b_ada.reshape(n_layers, 1, nc))


def _inproj_kernel(x_ref, shift_ref, scale_ref, w_ref, u_ref, rx_ref, gg_ref):
    n_pos, bsz, d = x_ref.shape
    h = x_ref[...] * (1.0 + scale_ref[...])[None, :, :] + shift_ref[...][None, :, :]
    h = h.reshape(n_pos * bsz, d).astype(BF16)
    z = jnp.dot(h, w_ref[...], preferred_element_type=F32)
    cw = z.shape[1] // 4
    tile = lambda v: v.reshape(n_pos, bsz, cw)
    u_ref[...] = tile(z[:, :cw] * jax.nn.sigmoid(z[:, cw:2 * cw]))
    rx_ref[...] = tile(z[:, 2 * cw:3 * cw])
    gg_ref[...] = tile(jax.nn.gelu(z[:, 3 * cw:]))


def _in_proj(x3, shift, scale, w_in_bf, n_pos):
    t, bsz, d = x3.shape
    cw = w_in_bf.shape[1] // 4
    tile = pl.BlockSpec((n_pos, bsz, cw), lambda i: (i, 0, 0))
    return pl.pallas_call(
        _inproj_kernel,
        name="in_proj",
        grid=(t // n_pos,),
        in_specs=[
            pl.BlockSpec((n_pos, bsz, d), lambda i: (i, 0, 0)),
            _const_spec((bsz, d)), _const_spec((bsz, d)),
            _const_spec(w_in_bf.shape),
        ],
        out_specs=[tile] * 3,
        out_shape=[jax.ShapeDtypeStruct((t, bsz, cw), F32)] * 3,
        compiler_params=_cparams("arbitrary"),
    )(x3, shift, scale, w_in_bf)


def _rows(ref):
    v = ref[...]
    return v.reshape(v.shape[0] * v.shape[1], v.shape[2])


def _short_conv(prev_ref, main_ref, next_ref, w_ref, b_ref, first, last):
    main = _rows(main_ref)
    prev = jnp.where(first, 0.0, _rows(prev_ref))
    nxt = jnp.where(last, 0.0, _rows(next_ref))
    ext = jnp.concatenate([prev, main, nxt], axis=0)
    n = main.shape[0]
    w = w_ref[...]
    acc = b_ref[...] + w[0:1] * ext[0:n]
    for k in range(1, w.shape[0]):
        acc = acc + w[k:k + 1] * ext[SUBLANES * k:SUBLANES * k + n]
    return acc


def _rglru_terms(xr, wax_ref, ba_ref, bx_ref, lam_ref):
    cw = xr.shape[1]
    g = jnp.dot(xr.astype(BF16), wax_ref[...], preferred_element_type=F32)
    r = jax.nn.sigmoid(g[:, :cw] + ba_ref[...])
    i = jax.nn.sigmoid(g[:, cw:] + bx_ref[...])
    nl = -lam_ref[...]
    softplus = jnp.maximum(nl, 0.0) + jnp.log1p(jnp.exp(-jnp.abs(nl)))
    log_a = -RG_C * r * softplus
    a = jnp.exp(log_a)
    th = jnp.tanh(log_a)
    u = jnp.sqrt(-2.0 * th / (1.0 - th)) * (i * xr)
    return a, u


def _scan_positions(a_scr, u_scr, write_h, h_start, n_pos, reverse):
    def body(j, h):
        p = (n_pos - 1 - j) if reverse else j
        rows = pl.ds(pl.multiple_of(p * SUBLANES, SUBLANES), SUBLANES)
        h = a_scr[rows, :] * h + u_scr[rows, :]
        write_h(p, rows, h)
        return h

    return lax.fori_loop(0, n_pos, body, h_start, unroll=4)


def _bwd_scan_kernel(rxp_ref, rxm_ref, rxn_ref, h0_ref, cw_ref, cb_ref, wax_ref, ba_ref, bx_ref, lam_ref,
                     hb_ref, hfin_ref, a_scr, u_scr, carry_scr):
    step = pl.program_id(0)
    n_t = pl.num_programs(0)
    tile = n_t - 1 - step
    n_pos = rxm_ref.shape[0]

    @pl.when(step == 0)
    def _():
        carry_scr[...] = h0_ref[...]

    xr = _short_conv(rxp_ref, rxm_ref, rxn_ref, cw_ref, cb_ref, tile == 0, tile == n_t - 1)
    a, u = _rglru_terms(xr, wax_ref, ba_ref, bx_ref, lam_ref)
    a_scr[...] = a
    u_scr[...] = u

    def write_h(p, rows, h):
        hb_ref[p] = h

    h = _scan_positions(a_scr, u_scr, write_h, carry_scr[...], n_pos, reverse=True)
    carry_scr[...] = h
    hfin_ref[...] = h


def _halo_specs(n_pos, halo_prev, halo_next, t_total, tile_of, tail):
    prev = pl.BlockSpec((halo_prev,) + tail,
                        lambda i: (jnp.maximum(tile_of(i) * (n_pos // halo_prev) - 1, 0), 0, 0))
    nxt = pl.BlockSpec((halo_next,) + tail,
                       lambda i: (jnp.minimum((tile_of(i) + 1) * (n_pos // halo_next), t_total // halo_next - 1),
                                  0, 0))
    return prev, nxt


def _bwd_scan(rx3, h0, p, n_pos):
    t, bsz, cw = rx3.shape
    n_t = t // n_pos
    rev = lambda i: n_t - 1 - i
    prev, nxt = _halo_specs(n_pos, 2, 1, t, rev, (bsz, cw))
    row = _const_spec((1, cw))
    state = _const_spec((bsz, cw))
    return pl.pallas_call(
        _bwd_scan_kernel,
        name="bwd_scan",
        grid=(n_t,),
        in_specs=[
            prev, pl.BlockSpec((n_pos, bsz, cw), lambda i: (rev(i), 0, 0)), nxt, state,
            _const_spec(p["rg_conv_w"].shape), row, _const_spec(p["wax_b"].shape), row, row, row,
        ],
        out_specs=[pl.BlockSpec((n_pos, bsz, cw), lambda i: (rev(i), 0, 0)), state],
        out_shape=[jax.ShapeDtypeStruct((t, bsz, cw), F32), jax.ShapeDtypeStruct((bsz, cw), F32)],
        scratch_shapes=[pltpu.VMEM((n_pos * bsz, cw), F32), pltpu.VMEM((n_pos * bsz, cw), F32),
                        pltpu.VMEM((bsz, cw), F32)],
        compiler_params=_cparams("arbitrary"),
    )(rx3, rx3, rx3, h0, p["rg_conv_w"], p["rg_conv_b"], p["wax_b"], p["ba_b"], p["bx_b"], p["lam_b"])


CONV_STRIP = 64


def _mix_out_kernel(x_ref, g1_ref, up_ref, um_ref, un_ref, rxp_ref, rxm_ref, rxn_ref, gg_ref, hb_ref, h0_ref,
                    cvw_ref, cvb_ref, cvg_ref, cvbeta_ref, cw_ref, cb_ref, wax_ref, ba_ref, bx_ref, lam_ref,
                    wo_ref, lng_ref, lnb_ref, o_ref, hfin_ref, ext_scr, cv_scr, a_scr, u_scr, hf_scr, carry_scr,
                    *, alpha):
    i = pl.program_id(0)
    n_t = pl.num_programs(0)
    first = i == 0
    last = i == n_t - 1
    n_pos, bsz, cw = um_ref.shape
    n_rows = n_pos * bsz
    halo_rows = CONV_HALO * bsz

    @pl.when(first)
    def _():
        carry_scr[...] = h0_ref[...]

    ext_scr[0:halo_rows] = jnp.where(first, 0.0, _rows(up_ref))
    ext_scr[halo_rows:halo_rows + n_rows] = _rows(um_ref)
    ext_scr[halo_rows + n_rows:] = jnp.where(last, 0.0, _rows(un_ref))
    n_taps = cvw_ref.shape[0]
    off = (CONV_HALO - (n_taps - 1) // 2) * bsz

    def conv_strip(s, carry):
        base = pl.multiple_of(s * CONV_STRIP, CONV_STRIP)
        for cb in range(0, cw, LANES):
            lanes = pl.ds(cb, LANES)
            acc = cvb_ref[:, lanes] + cvw_ref[0:1, lanes] * ext_scr[pl.ds(base + off, CONV_STRIP), lanes]
            for k in range(1, n_taps):
                acc = acc + cvw_ref[k:k + 1, lanes] * ext_scr[pl.ds(base + off + k * bsz, CONV_STRIP), lanes]
            cv_scr[pl.ds(base, CONV_STRIP), lanes] = acc
        return carry

    lax.fori_loop(0, n_rows // CONV_STRIP, conv_strip, 0)
    v = _layer_norm(cv_scr[...], cvg_ref[...], cvbeta_ref[...])
    y_cv = v * jax.nn.sigmoid(v)

    xr = _short_conv(rxp_ref, rxm_ref, rxn_ref, cw_ref, cb_ref, first, last)
    a, u = _rglru_terms(xr, wax_ref, ba_ref, bx_ref, lam_ref)
    a_scr[...] = a
    u_scr[...] = u

    def write_h(p, rows, h):
        hf_scr[rows, :] = h

    h = _scan_positions(a_scr, u_scr, write_h, carry_scr[...], n_pos, reverse=False)
    carry_scr[...] = h
    hfin_ref[...] = h
    y_rg = (hf_scr[...] + _rows(hb_ref)) * _rows(gg_ref)

    m = jnp.dot(y_cv.astype(BF16), wo_ref[0:cw, :], preferred_element_type=F32)
    m += jnp.dot(y_rg.astype(BF16), wo_ref[cw:, :], preferred_element_type=F32)
    d = m.shape[1]
    gate = g1_ref[...]
    v = alpha * x_ref[...] + gate[None, :, :] * m.reshape(n_pos, bsz, d)
    o_ref[...] = _layer_norm(v, lng_ref[...], lnb_ref[...])


def _mix_out(x3, g1, u3, rx3, gg3, hb3, h0, p, n_pos, alpha):
    t, bsz, d = x3.shape
    cw = rx3.shape[2]
    n_t = t // n_pos
    ident = lambda i: i
    up, un = _halo_specs(n_pos, CONV_HALO, CONV_HALO, t, ident, (bsz, cw))
    rxp, rxn = _halo_specs(n_pos, 2, 1, t, ident, (bsz, cw))
    tile = pl.BlockSpec((n_pos, bsz, cw), lambda i: (i, 0, 0))
    xtile = pl.BlockSpec((n_pos, bsz, d), lambda i: (i, 0, 0))
    state = _const_spec((bsz, cw))
    row = _const_spec((1, cw))
    drow = _const_spec((1, d))
    n_rows = n_pos * bsz
    return pl.pallas_call(
        functools.partial(_mix_out_kernel, alpha=alpha),
        name="mix_out",
        grid=(n_t,),
        in_specs=[
            xtile, _const_spec((bsz, d)),
            up, tile, un, rxp, tile, rxn, tile, tile, state,
            _const_spec(p["conv_w"].shape), row, row, row,
            _const_spec(p["rg_conv_w"].shape), row, _const_spec(p["wax_f"].shape), row, row, row,
            _const_spec(p["w_out"].shape), drow, drow,
        ],
        out_specs=[xtile, state],
        out_shape=[jax.ShapeDtypeStruct((t, bsz, d), F32), jax.ShapeDtypeStruct((bsz, cw), F32)],
        scratch_shapes=[
            pltpu.VMEM((n_rows + 2 * CONV_HALO * bsz, cw), F32),
            pltpu.VMEM((n_rows, cw), F32),
            pltpu.VMEM((n_rows, cw), F32), pltpu.VMEM((n_rows, cw), F32),
            pltpu.VMEM((n_rows, cw), F32),
            pltpu.VMEM((bsz, cw), F32),
        ],
        compiler_params=_cparams("arbitrary"),
    )(x3, g1, u3, u3, u3, rx3, rx3, rx3, gg3, hb3, h0,
      p["conv_w"], p["conv_b"], p["conv_ln_g"], p["conv_ln_b"],
      p["rg_conv_w"], p["rg_conv_b"], p["wax_f"], p["ba_f"], p["bx_f"], p["lam_f"],
      p["w_out"], p["ln1_g"], p["ln1_b"])


def _grid_conv_row(z_ref, col0, row0, width, w9, bias, multi_row):
    lanes = pl.ds(col0, LANES)
    if multi_row:
        zt = z_ref[pl.ds(row0, width), lanes]
        zm = z_ref[pl.ds(row0 + width, width), lanes]
        zb = z_ref[pl.ds(row0 + 2 * width, width), lanes]
        v = [w9[dj] * zt + w9[3 + dj] * zm + w9[6 + dj] * zb for dj in range(3)]
    else:
        zm = z_ref[pl.ds(row0, width), lanes]
        v = [w9[3 + dj] * zm for dj in range(3)]
    pos = lax.broadcasted_iota(jnp.int32, (width, LANES), 0)
    left = jnp.where(pos == 0, 0.0, pltpu.roll(v[0], 1, axis=0))
    right = jnp.where(pos == width - 1, 0.0, pltpu.roll(v[2], width - 1, axis=0))
    return (v[1] + bias) + (left + right)


def _ffn_kernel(*refs, width, multi_row, n_groups, alpha):
    if multi_row:
        xp_ref, xm_ref, xn_ref = refs[:3]
        refs = refs[3:]
    else:
        xm_ref = refs[0]
        refs = refs[1:]
    mod_ref, wu_ref, dw_ref, db_ref, wd_ref, lng_ref, lnb_ref, o_ref, h_scr, za_scr, zb_scr, y_scr, acc_scr = refs
    n_out = xm_ref.shape[0]
    n_f, fc, _ = wd_ref.shape
    off = width if multi_row else 0

    shift = mod_ref[0, 3:4, :]
    scale = mod_ref[0, 4:5, :]
    h_scr[off:off + n_out] = (xm_ref[...] * (1.0 + scale) + shift).astype(BF16)
    if multi_row:
        r = pl.program_id(1)
        n_r = pl.num_programs(1)
        hp = xp_ref[...] * (1.0 + scale) + shift
        hn = xn_ref[...] * (1.0 + scale) + shift
        h_scr[0:width] = jnp.where(r == 0, 0.0, hp).astype(BF16)
        h_scr[off + n_out:] = jnp.where(r == n_r - 1, 0.0, hn).astype(BF16)
    acc_scr[...] = jnp.zeros_like(acc_scr)

    m_ext = h_scr.shape[0]
    up_rows = m_ext // n_groups
    out_rows = n_out // n_groups

    z_bufs = (za_scr, zb_scr)

    def up_project(f, slot, g):
        z_scr = z_bufs[slot]
        rows = pl.ds(pl.multiple_of(g * up_rows, 16), up_rows)
        h = h_scr[rows, :]
        z_scr[rows, 0:fc] = jnp.dot(h, wu_ref[f], preferred_element_type=F32)
        z_scr[rows, fc:2 * fc] = jnp.dot(h, wu_ref[n_f + f], preferred_element_type=F32)

    def conv_gate_down(f, slot, g):
        z_scr = z_bufs[slot]
        base = pl.multiple_of(g * out_rows, out_rows)
        for cb in range(0, fc, LANES):
            wu9 = [dw_ref[f, k:k + 1, cb:cb + LANES] for k in range(9)]
            wg9 = [dw_ref[n_f + f, k:k + 1, cb:cb + LANES] for k in range(9)]
            bu = db_ref[f, :, cb:cb + LANES]
            bg = db_ref[n_f + f, :, cb:cb + LANES]
            for r0 in range(0, out_rows, width):
                cu = _grid_conv_row(z_scr, cb, base + r0, width, wu9, bu, multi_row)
                cg = _grid_conv_row(z_scr, fc + cb, base + r0, width, wg9, bg, multi_row)
                y_scr[pl.ds(r0, width), pl.ds(cb, LANES)] = (cu * jax.nn.gelu(cg)).astype(BF16)
        rows = pl.ds(base, out_rows)
        acc_scr[rows, :] += jnp.dot(y_scr[...], wd_ref[f], preferred_element_type=F32)

    def groups(body):
        lax.fori_loop(0, n_groups, lambda g, c: (body(g), c)[1], 0)

    groups(lambda g: up_project(0, 0, g))

    def stage(f, slot):
        def body(g):
            up_project(f + 1, 1 - slot, g)
            conv_gate_down(f, slot, g)
        groups(body)

    def chunk_pair(j, carry):
        stage(2 * j, 0)
        stage(2 * j + 1, 1)
        return carry

    lax.fori_loop(0, (n_f - 1) // 2, chunk_pair, 0)
    if (n_f - 1) % 2:
        stage(n_f - 2, 0)
    groups(lambda g: conv_gate_down(n_f - 1, (n_f - 1) % 2, g))

    gate = mod_ref[0, 5:6, :]
    out = _layer_norm(alpha * xm_ref[...] + gate * acc_scr[...], lng_ref[...], lnb_ref[...])
    o_ref[...] = out.reshape(o_ref.shape)


def _conv_ffn(x2, mod, p, width, rows_per_tile, alpha, batch_major_out=False):
    bsz, _, d = mod.shape
    t = x2.shape[0]
    n_out = rows_per_tile * width
    n_r = t // n_out
    multi_row = t > width
    n_f, fc, _ = p["ffn_down"].shape
    last_row = t // width - 1
    main = pl.BlockSpec((n_out, d), lambda b, r: (r, b))
    x_specs = [main]
    x_args = [x2]
    if multi_row:
        x_specs = [
            pl.BlockSpec((width, d), lambda b, r: (jnp.maximum(r * rows_per_tile - 1, 0), b)),
            main,
            pl.BlockSpec((width, d), lambda b, r: (jnp.minimum((r + 1) * rows_per_tile, last_row), b)),
        ]
        x_args = [x2, x2, x2]
    m_ext = n_out + (2 * width if multi_row else 0)
    n_groups = _ffn_row_groups(rows_per_tile, m_ext)
    drow = _const_spec((1, d))
    return pl.pallas_call(
        functools.partial(_ffn_kernel, width=width, multi_row=multi_row, n_groups=n_groups, alpha=alpha),
        name="conv_ffn",
        grid=(bsz, n_r),
        in_specs=x_specs + [
            pl.BlockSpec((1, N_MOD, d), lambda b, r: (b, 0, 0)),
            _resident_spec(p["ffn_up"].shape), _resident_spec(p["ffn_dw"].shape),
            _resident_spec(p["ffn_dw_b"].shape), _resident_spec(p["ffn_down"].shape),
            drow, drow,
        ],
        out_specs=pl.BlockSpec((1, n_out, d), lambda b, r: (b, r, 0)) if batch_major_out else main,
        out_shape=jax.ShapeDtypeStruct((bsz, t, d) if batch_major_out else (t, bsz * d), F32),
        scratch_shapes=[
            pltpu.VMEM((m_ext, d), BF16),
            pltpu.VMEM((m_ext, 2 * fc), F32),
            pltpu.VMEM((m_ext, 2 * fc), F32),
            pltpu.VMEM((n_out // n_groups, fc), BF16),
            pltpu.VMEM((n_out, d), F32),
        ],
        compiler_params=_cparams("arbitrary", "arbitrary"),
    )(*x_args, mod, p["ffn_up"], p["ffn_dw"], p["ffn_dw_b"], p["ffn_down"], p["ln2_g"], p["ln2_b"])


def _seq_positions(t):
    return 64 if t % 64 == 0 else t


def _ffn_rows_per_tile(rows):
    return 16 if rows % 16 == 0 else rows


def _ffn_row_groups(rows_per_tile, m_ext):
    n = max(rows_per_tile // 8, 1)
    while m_ext % (16 * n) or rows_per_tile % n:
        n -= 1
    return n


def _ffn_chunk(f_dim):
    return 256 if f_dim % 256 == 0 else 128


def _block_diag_dense(w):
    nh, dh, _ = w.shape
    eye = jnp.eye(nh, dtype=w.dtype)
    return jnp.einsum("hij,hg->higj", w, eye).reshape(nh * dh, nh * dh)


def _layer_params(l, w_in, conv_w, conv_b, conv_ln_g, conv_ln_b, rg_conv_w, rg_conv_b, rg_wa, rg_ba, rg_wx,
                  rg_bx, rg_lambda, w_out, ln1_g, ln1_b, ffn_up, ffn_dw, ffn_dw_b, ffn_down, ln2_g, ln2_b):
    row = lambda v: v.reshape(1, -1)
    p = {
        "w_in": w_in[l].astype(BF16),
        "conv_w": conv_w[l], "conv_b": row(conv_b[l]),
        "conv_ln_g": row(conv_ln_g[l]), "conv_ln_b": row(conv_ln_b[l]),
        "rg_conv_w": rg_conv_w[l], "rg_conv_b": row(rg_conv_b[l]),
        "w_out": w_out[l].astype(BF16), "ln1_g": row(ln1_g[l]), "ln1_b": row(ln1_b[l]),
        "ln2_g": row(ln2_g[l]), "ln2_b": row(ln2_b[l]),
    }
    f_dim, d = ffn_down.shape[1:]
    fc = _ffn_chunk(f_dim)
    n_c = 2 * f_dim // fc
    p["ffn_up"] = ffn_up[l].astype(BF16).reshape(d, n_c, fc).transpose(1, 0, 2)
    p["ffn_dw"] = ffn_dw[l].reshape(9, n_c, fc).transpose(1, 0, 2)
    p["ffn_dw_b"] = ffn_dw_b[l].reshape(n_c, 1, fc)
    p["ffn_down"] = ffn_down[l].astype(BF16).reshape(f_dim // fc, fc, d)
    for k, name in ((0, "f"), (1, "b")):
        p["wax_" + name] = jnp.concatenate(
            [_block_diag_dense(rg_wa[l, k]), _block_diag_dense(rg_wx[l, k])], axis=1).astype(BF16)
        p["ba_" + name] = row(rg_ba[l, k])
        p["bx_" + name] = row(rg_bx[l, k])
        p["lam_" + name] = row(rg_lambda[l, k])
    return p


def _mixer(x2, mod, p, h0_f, h0_b, alpha):
    bsz, _, d = mod.shape
    t = x2.shape[0]
    x3 = x2.reshape(t, bsz, d)
    n_pos = _seq_positions(t)
    u3, rx3, gg3 = _in_proj(x3, mod[:, 0, :], mod[:, 1, :], p["w_in"], n_pos)
    hb3, hb_fin = _bwd_scan(rx3, h0_b, p, n_pos)
    x_new, hf_fin = _mix_out(x3, mod[:, 2, :], u3, rx3, gg3, hb3, h0_f, p, n_pos, alpha)
    return x_new.reshape(t, bsz * d), hf_fin, hb_fin


def kernel(x, c, ctx, c_ctx, w_ada, b_ada, w_in, conv_w, conv_b, conv_ln_g, conv_ln_b, rg_conv_w, rg_conv_b,
           rg_wa, rg_ba, rg_wx, rg_bx, rg_lambda, w_out, ln1_g, ln1_b, ffn_up, ffn_dw, ffn_dw_b, ffn_down,
           ln2_g, ln2_b):
    bsz, seq, d = x.shape
    ctx_len = ctx.shape[1]
    depth = w_in.shape[0]
    alpha = (2.0 * depth) ** 0.25
    cw = rg_lambda.shape[-1]
    assert bsz == SUBLANES, "the sequence kernels put the batch on the 8 sublanes of a vreg"

    pad = (-(bsz + 1)) % 8
    cc = jnp.concatenate([c, c_ctx[None, :], jnp.zeros((pad, d), F32)], axis=0)
    mods = _ada_rows(cc, w_ada, b_ada).reshape(depth, bsz + 1 + pad, N_MOD, d)
    zeros_state = jnp.zeros((bsz, cw), F32)

    xs = jnp.transpose(x, (1, 0, 2)).reshape(seq, bsz * d)
    cs = jnp.transpose(ctx, (1, 0, 2)).reshape(ctx_len, bsz * d)
    rows = seq // LATENT_GRID_W

    for l in range(depth):
        p = _layer_params(l, w_in, conv_w, conv_b, conv_ln_g, conv_ln_b, rg_conv_w, rg_conv_b, rg_wa, rg_ba,
                          rg_wx, rg_bx, rg_lambda, w_out, ln1_g, ln1_b, ffn_up, ffn_dw, ffn_dw_b, ffn_down,
                          ln2_g, ln2_b)
        mod_lat = mods[l, :bsz]
        mod_ctx = jnp.broadcast_to(mods[l, bsz:bsz + 1], (bsz, N_MOD, d))
        cs_mid, hc_f, hc_b = _mixer(cs, mod_ctx, p, zeros_state, zeros_state, alpha)
        xs, _, _ = _mixer(xs, mod_lat, p, hc_f, hc_b, alpha)
        xs = _conv_ffn(xs, mod_lat, p, LATENT_GRID_W, _ffn_rows_per_tile(rows), alpha,
                       batch_major_out=(l == depth - 1))
        if l < depth - 1:
            cs = _conv_ffn(cs_mid, mod_ctx, p, ctx_len, 1, alpha)
    return xs
```

```python
import functools

import jax
import jax.numpy as jnp
from jax import lax
from jax.experimental import pallas as pl
from jax.experimental.pallas import tpu as pltpu

F32 = jnp.float32
BF16 = jnp.bfloat16

LN_EPS = 1e-5
RG_C = 8.0
N_MOD = 6
LATENT_GRID_W = 64
CONV_HALO = 16
VMEM_LIMIT_BYTES = 56 * 1024 * 1024
LANES = 128
SUBLANES = 8


def _cparams(*sem):
    return pltpu.CompilerParams(dimension_semantics=sem, vmem_limit_bytes=VMEM_LIMIT_BYTES)


def _const_spec(shape):
    nd = len(shape)
    return pl.BlockSpec(shape, lambda *_: (0,) * nd)


def _resident_spec(shape):
    nd = len(shape)
    return pl.BlockSpec(shape, lambda *_: (0,) * nd, pipeline_mode=pl.Buffered(1))


def _layer_norm(v, g, b):
    mu = jnp.mean(v, axis=-1, keepdims=True)
    d = v - mu
    var = jnp.mean(d * d, axis=-1, keepdims=True)
    return d * lax.rsqrt(var + LN_EPS) * g + b


def _split_bf16(v):
    hi = v.astype(BF16)
    lo = (v - hi.astype(F32)).astype(BF16)
    return hi, lo


def _ada_kernel(c_ref, w_ref, b_ref, o_ref):
    c = c_ref[...]
    s = c * jax.nn.sigmoid(c)
    s_hi, s_lo = _split_bf16(s)
    w_hi, w_lo = _split_bf16(w_ref[0])
    acc = jnp.dot(s_hi, w_hi, preferred_element_type=F32)
    acc += jnp.dot(s_hi, w_lo, preferred_element_type=F32)
    acc += jnp.dot(s_lo, w_hi, preferred_element_type=F32)
    o_ref[0] = acc + b_ref[0]


def _ada_rows(cc, w_ada, b_ada):
    n_layers, d, nc = w_ada.shape
    rows = cc.shape[0]
    tn = 1536
    return pl.pallas_call(
        _ada_kernel,
        name="ada_rows",
        grid=(n_layers, nc // tn),
        in_specs=[
            pl.BlockSpec((rows, d), lambda l, j: (0, 0)),
            pl.BlockSpec((1, d, tn), lambda l, j: (l, 0, j)),
            pl.BlockSpec((1, 1, tn), lambda l, j: (l, 0, j)),
        ],
        out_specs=pl.BlockSpec((1, rows, tn), lambda l, j: (l, 0, j)),
        out_shape=jax.ShapeDtypeStruct((n_layers, rows, nc), F32),
        compiler_params=_cparams("arbitrary", "arbitrary"),
    )(cc, w_ada, b_ada.reshape(n_layers, 1, nc))


def _inproj_kernel(x_ref, shift_ref, scale_ref, w_ref, u_ref, rx_ref, gg_ref):
    n_pos, bsz, d = x_ref.shape
    h = x_ref[...] * (1.0 + scale_ref[...])[None, :, :] + shift_ref[...][None, :, :]
    h = h.reshape(n_pos * bsz, d).astype(BF16)
    z = jnp.dot(h, w_ref[...], preferred_element_type=F32)
    cw = z.shape[1] // 4
    tile = lambda v: v.reshape(n_pos, bsz, cw)
    u_ref[...] = tile(z[:, :cw] * jax.nn.sigmoid(z[:, cw:2 * cw]))
    rx_ref[...] = tile(z[:, 2 * cw:3 * cw])
    gg_ref[...] = tile(jax.nn.gelu(z[:, 3 * cw:]))


def _in_proj(x3, shift, scale, w_in_bf, n_pos):
    t, bsz, d = x3.shape
    cw = w_in_bf.shape[1] // 4
    tile = pl.BlockSpec((n_pos, bsz, cw), lambda i: (i, 0, 0))
    return pl.pallas_call(
        _inproj_kernel,
        name="in_proj",
        grid=(t // n_pos,),
        in_specs=[
            pl.BlockSpec((n_pos, bsz, d), lambda i: (i, 0, 0)),
            _const_spec((bsz, d)), _const_spec((bsz, d)),
            _const_spec(w_in_bf.shape),
        ],
        out_specs=[tile] * 3,
        out_shape=[jax.ShapeDtypeStruct((t, bsz, cw), F32)] * 3,
        compiler_params=_cparams("arbitrary"),
    )(x3, shift, scale, w_in_bf)


def _rows(ref):
    v = ref[...]
    return v.reshape(v.shape[0] * v.shape[1], v.shape[2])


def _short_conv(prev_ref, main_ref, next_ref, w_ref, b_ref, first, last):
    main = _rows(main_ref)
    prev = jnp.where(first, 0.0, _rows(prev_ref))
    nxt = jnp.where(last, 0.0, _rows(next_ref))
    ext = jnp.concatenate([prev, main, nxt], axis=0)
    n = main.shape[0]
    w = w_ref[...]
    acc = b_ref[...] + w[0:1] * ext[0:n]
    for k in range(1, w.shape[0]):
        acc = acc + w[k:k + 1] * ext[SUBLANES * k:SUBLANES * k + n]
    return acc


def _rglru_terms(xr, wax_ref, ba_ref, bx_ref, lam_ref):
    cw = xr.shape[1]
    g = jnp.dot(xr.astype(BF16), wax_ref[...], preferred_element_type=F32)
    r = jax.nn.sigmoid(g[:, :cw] + ba_ref[...])
    i = jax.nn.sigmoid(g[:, cw:] + bx_ref[...])
    nl = -lam_ref[...]
    softplus = jnp.maximum(nl, 0.0) + jnp.log1p(jnp.exp(-jnp.abs(nl)))
    log_a = -RG_C * r * softplus
    a = jnp.exp(log_a)
    th = jnp.tanh(log_a)
    u = jnp.sqrt(-2.0 * th / (1.0 - th)) * (i * xr)
    return a, u


def _scan_positions(a_scr, u_scr, write_h, h_start, n_pos, reverse):
    def body(j, h):
        p = (n_pos - 1 - j) if reverse else j
        rows = pl.ds(pl.multiple_of(p * SUBLANES, SUBLANES), SUBLANES)
        h = a_scr[rows, :] * h + u_scr[rows, :]
        write_h(p, rows, h)
        return h

    return lax.fori_loop(0, n_pos, body, h_start, unroll=4)


def _bwd_scan_kernel(rxp_ref, rxm_ref, rxn_ref, h0_ref, cw_ref, cb_ref, wax_ref, ba_ref, bx_ref, lam_ref,
                     hb_ref, hfin_ref, a_scr, u_scr, carry_scr):
    step = pl.program_id(0)
    n_t = pl.num_programs(0)
    tile = n_t - 1 - step
    n_pos = rxm_ref.shape[0]

    @pl.when(step == 0)
    def _():
        carry_scr[...] = h0_ref[...]

    xr = _short_conv(rxp_ref, rxm_ref, rxn_ref, cw_ref, cb_ref, tile == 0, tile == n_t - 1)
    a, u = _rglru_terms(xr, wax_ref, ba_ref, bx_ref, lam_ref)
    a_scr[...] = a
    u_scr[...] = u

    def write_h(p, rows, h):
        hb_ref[p] = h

    h = _scan_positions(a_scr, u_scr, write_h, carry_scr[...], n_pos, reverse=True)
    carry_scr[...] = h
    hfin_ref[...] = h


def _halo_specs(n_pos, halo_prev, halo_next, t_total, tile_of, tail):
    prev = pl.BlockSpec((halo_prev,) + tail,
                        lambda i: (jnp.maximum(tile_of(i) * (n_pos // halo_prev) - 1, 0), 0, 0))
    nxt = pl.BlockSpec((halo_next,) + tail,
                       lambda i: (jnp.minimum((tile_of(i) + 1) * (n_pos // halo_next), t_total // halo_next - 1),
                                  0, 0))
    return prev, nxt


def _bwd_scan(rx3, h0, p, n_pos):
    t, bsz, cw = rx3.shape
    n_t = t // n_pos
    rev = lambda i: n_t - 1 - i
    prev, nxt = _halo_specs(n_pos, 2, 1, t, rev, (bsz, cw))
    row = _const_spec((1, cw))
    state = _const_spec((bsz, cw))
    return pl.pallas_call(
        _bwd_scan_kernel,
        name="bwd_scan",
        grid=(n_t,),
        in_specs=[
            prev, pl.BlockSpec((n_pos, bsz, cw), lambda i: (rev(i), 0, 0)), nxt, state,
            _const_spec(p["rg_conv_w"].shape), row, _const_spec(p["wax_b"].shape), row, row, row,
        ],
        out_specs=[pl.BlockSpec((n_pos, bsz, cw), lambda i: (rev(i), 0, 0)), state],
        out_shape=[jax.ShapeDtypeStruct((t, bsz, cw), F32), jax.ShapeDtypeStruct((bsz, cw), F32)],
        scratch_shapes=[pltpu.VMEM((n_pos * bsz, cw), F32), pltpu.VMEM((n_pos * bsz, cw), F32),
                        pltpu.VMEM((bsz, cw), F32)],
        compiler_params=_cparams("arbitrary"),
    )(rx3, rx3, rx3, h0, p["rg_conv_w"], p["rg_conv_b"], p["wax_b"], p["ba_b"], p["bx_b"], p["lam_b"])


CONV_STRIP = 64


def _mix_out_kernel(x_ref, g1_ref, up_ref, um_ref, un_ref, rxp_ref, rxm_ref, rxn_ref, gg_ref, hb_ref, h0_ref,
                    cvw_ref, cvb_ref, cvg_ref, cvbeta_ref, cw_ref, cb_ref, wax_ref, ba_ref, bx_ref, lam_ref,
                    wo_ref, lng_ref, lnb_ref, o_ref, hfin_ref, ext_scr, cv_scr, a_scr, u_scr, hf_scr, carry_scr,
                    *, alpha):
    i = pl.program_id(0)
    n_t = pl.num_programs(0)
    first = i == 0
    last = i == n_t - 1
    n_pos, bsz, cw = um_ref.shape
    n_rows = n_pos * bsz
    halo_rows = CONV_HALO * bsz

    @pl.when(first)
    def _():
        carry_scr[...] = h0_ref[...]

    ext_prev = jnp.where(first, 0.0, _rows(up_ref))
    ext_main = _rows(um_ref)
    ext_next = jnp.where(last, 0.0, _rows(un_ref))
    for j in range(cw // LANES):
        ext_scr[j, 0:halo_rows, :] = ext_prev[:, j * LANES:(j + 1) * LANES]
        ext_scr[j, halo_rows:halo_rows + n_rows, :] = ext_main[:, j * LANES:(j + 1) * LANES]
        ext_scr[j, halo_rows + n_rows:, :] = ext_next[:, j * LANES:(j + 1) * LANES]
    n_taps = cvw_ref.shape[0]
    off = (CONV_HALO - (n_taps - 1) // 2) * bsz

    for j in range(cw // LANES):
        lanes = pl.ds(j * LANES, LANES)

        def conv_strip(s, carry, j=j, lanes=lanes):
            base = pl.multiple_of(s * CONV_STRIP, CONV_STRIP)
            parts = [None] * 4
            for k in range(n_taps):
                term = cvw_ref[k:k + 1, lanes] * ext_scr[j, pl.ds(base + off + k * bsz, CONV_STRIP), :]
                parts[k % 4] = term if parts[k % 4] is None else parts[k % 4] + term
            cv_scr[pl.ds(base, CONV_STRIP), lanes] = ((parts[0] + parts[1]) + (parts[2] + parts[3])) + cvb_ref[:, lanes]
            return carry

        lax.fori_loop(0, n_rows // CONV_STRIP, conv_strip, 0)
    v = _layer_norm(cv_scr[...], cvg_ref[...], cvbeta_ref[...])
    y_cv = v * jax.nn.sigmoid(v)

    xr = _short_conv(rxp_ref, rxm_ref, rxn_ref, cw_ref, cb_ref, first, last)
    a, u = _rglru_terms(xr, wax_ref, ba_ref, bx_ref, lam_ref)
    a_scr[...] = a
    u_scr[...] = u

    def write_h(p, rows, h):
        hf_scr[rows, :] = h

    h = _scan_positions(a_scr, u_scr, write_h, carry_scr[...], n_pos, reverse=False)
    carry_scr[...] = h
    hfin_ref[...] = h
    y_rg = (hf_scr[...] + _rows(hb_ref)) * _rows(gg_ref)

    m = jnp.dot(y_cv.astype(BF16), wo_ref[0:cw, :], preferred_element_type=F32)
    m += jnp.dot(y_rg.astype(BF16), wo_ref[cw:, :], preferred_element_type=F32)
    d = m.shape[1]
    gate = g1_ref[...]
    v = alpha * x_ref[...] + gate[None, :, :] * m.reshape(n_pos, bsz, d)
    o_ref[...] = _layer_norm(v, lng_ref[...], lnb_ref[...])


def _mix_out(x3, g1, u3, rx3, gg3, hb3, h0, p, n_pos, alpha):
    t, bsz, d = x3.shape
    cw = rx3.shape[2]
    n_t = t // n_pos
    ident = lambda i: i
    up, un = _halo_specs(n_pos, CONV_HALO, CONV_HALO, t, ident, (bsz, cw))
    rxp, rxn = _halo_specs(n_pos, 2, 1, t, ident, (bsz, cw))
    tile = pl.BlockSpec((n_pos, bsz, cw), lambda i: (i, 0, 0))
    xtile = pl.BlockSpec((n_pos, bsz, d), lambda i: (i, 0, 0))
    state = _const_spec((bsz, cw))
    row = _const_spec((1, cw))
    drow = _const_spec((1, d))
    n_rows = n_pos * bsz
    return pl.pallas_call(
        functools.partial(_mix_out_kernel, alpha=alpha),
        name="mix_out",
        grid=(n_t,),
        in_specs=[
            xtile, _const_spec((bsz, d)),
            up, tile, un, rxp, tile, rxn, tile, tile, state,
            _const_spec(p["conv_w"].shape), row, row, row,
            _const_spec(p["rg_conv_w"].shape), row, _const_spec(p["wax_f"].shape), row, row, row,
            _const_spec(p["w_out"].shape), drow, drow,
        ],
        out_specs=[xtile, state],
        out_shape=[jax.ShapeDtypeStruct((t, bsz, d), F32), jax.ShapeDtypeStruct((bsz, cw), F32)],
        scratch_shapes=[
            pltpu.VMEM((cw // LANES, n_rows + 2 * CONV_HALO * bsz, LANES), F32),
            pltpu.VMEM((n_rows, cw), F32),
            pltpu.VMEM((n_rows, cw), F32), pltpu.VMEM((n_rows, cw), F32),
            pltpu.VMEM((n_rows, cw), F32),
            pltpu.VMEM((bsz, cw), F32),
        ],
        compiler_params=_cparams("arbitrary"),
    )(x3, g1, u3, u3, u3, rx3, rx3, rx3, gg3, hb3, h0,
      p["conv_w"], p["conv_b"], p["conv_ln_g"], p["conv_ln_b"],
      p["rg_conv_w"], p["rg_conv_b"], p["wax_f"], p["ba_f"], p["bx_f"], p["lam_f"],
      p["w_out"], p["ln1_g"], p["ln1_b"])


def _grid_conv_row(z_ref, col0, row0, width, w9, bias, multi_row):
    lanes = pl.ds(col0, LANES)
    if multi_row:
        zt = z_ref[pl.ds(row0, width), lanes]
        zm = z_ref[pl.ds(row0 + width, width), lanes]
        zb = z_ref[pl.ds(row0 + 2 * width, width), lanes]
        v = [w9[dj] * zt + w9[3 + dj] * zm + w9[6 + dj] * zb for dj in range(3)]
    else:
        zm = z_ref[pl.ds(row0, width), lanes]
        v = [w9[3 + dj] * zm for dj in range(3)]
    pos = lax.broadcasted_iota(jnp.int32, (width, LANES), 0)
    left = jnp.where(pos == 0, 0.0, pltpu.roll(v[0], 1, axis=0))
    right = jnp.where(pos == width - 1, 0.0, pltpu.roll(v[2], width - 1, axis=0))
    return (v[1] + bias) + (left + right)


def _ffn_kernel(*refs, width, multi_row, n_groups, alpha):
    if multi_row:
        xp_ref, xm_ref, xn_ref = refs[:3]
        refs = refs[3:]
    else:
        xm_ref = refs[0]
        refs = refs[1:]
    mod_ref, wu_ref, dw_ref, db_ref, wd_ref, lng_ref, lnb_ref, o_ref, h_scr, za_scr, zb_scr, y_scr = refs
    n_out = xm_ref.shape[0]
    fc = db_ref.shape[2]
    n_f = wd_ref.shape[0] // fc
    off = width if multi_row else 0

    shift = mod_ref[0, 3:4, :]
    scale = mod_ref[0, 4:5, :]
    h_scr[off:off + n_out] = (xm_ref[...] * (1.0 + scale) + shift).astype(BF16)
    if multi_row:
        r = pl.program_id(1)
        n_r = pl.num_programs(1)
        hp = xp_ref[...] * (1.0 + scale) + shift
        hn = xn_ref[...] * (1.0 + scale) + shift
        h_scr[0:width] = jnp.where(r == 0, 0.0, hp).astype(BF16)
        h_scr[off + n_out:] = jnp.where(r == n_r - 1, 0.0, hn).astype(BF16)

    m_ext = h_scr.shape[0]
    up_rows = m_ext // n_groups
    out_rows = n_out // n_groups

    z_bufs = (za_scr, zb_scr)

    def up_project(f, slot, g):
        z_scr = z_bufs[slot]
        rows = pl.ds(pl.multiple_of(g * up_rows, 16), up_rows)
        h = h_scr[rows, :]
        z_scr[rows, 0:fc] = jnp.dot(h, wu_ref[f], preferred_element_type=F32)
        z_scr[rows, fc:2 * fc] = jnp.dot(h, wu_ref[n_f + f], preferred_element_type=F32)

    def conv_gate_down(f, slot, g):
        z_scr = z_bufs[slot]
        base = pl.multiple_of(g * out_rows, out_rows)
        for cb in range(0, fc, LANES):
            wu9 = [dw_ref[f, k:k + 1, cb:cb + LANES] for k in range(9)]
            wg9 = [dw_ref[n_f + f, k:k + 1, cb:cb + LANES] for k in range(9)]
            bu = db_ref[f, :, cb:cb + LANES]
            bg = db_ref[n_f + f, :, cb:cb + LANES]
            for r0 in range(0, out_rows, width):
                cu = _grid_conv_row(z_scr, cb, base + r0, width, wu9, bu, multi_row)
                cg = _grid_conv_row(z_scr, fc + cb, base + r0, width, wg9, bg, multi_row)
                cols = pl.ds(pl.multiple_of(f * fc, fc) + cb, LANES)
                y_scr[pl.ds(base + r0, width), cols] = (cu * jax.nn.gelu(cg)).astype(BF16)

    def groups(body):
        lax.fori_loop(0, n_groups, lambda g, c: (body(g), c)[1], 0)

    groups(lambda g: up_project(0, 0, g))

    def stage(f, slot):
        def body(g):
            up_project(f + 1, 1 - slot, g)
            conv_gate_down(f, slot, g)
        groups(body)

    def chunk_pair(j, carry):
        stage(2 * j, 0)
        stage(2 * j + 1, 1)
        return carry

    lax.fori_loop(0, (n_f - 1) // 2, chunk_pair, 0)
    if (n_f - 1) % 2:
        stage(n_f - 2, 0)
    groups(lambda g: conv_gate_down(n_f - 1, (n_f - 1) % 2, g))

    ffn = jnp.dot(y_scr[...], wd_ref[...], preferred_element_type=F32)
    gate = mod_ref[0, 5:6, :]
    out = _layer_norm(alpha * xm_ref[...] + gate * ffn, lng_ref[...], lnb_ref[...])
    o_ref[...] = out.reshape(o_ref.shape)


def _conv_ffn(x2, mod, p, width, rows_per_tile, alpha, batch_major_out=False):
    bsz, _, d = mod.shape
    t = x2.shape[0]
    n_out = rows_per_tile * width
    n_r = t // n_out
    multi_row = t > width
    fc = p["ffn_dw_b"].shape[2]
    last_row = t // width - 1
    main = pl.BlockSpec((n_out, d), lambda b, r: (r, b))
    x_specs = [main]
    x_args = [x2]
    if multi_row:
        x_specs = [
            pl.BlockSpec((width, d), lambda b, r: (jnp.maximum(r * rows_per_tile - 1, 0), b)),
            main,
            pl.BlockSpec((width, d), lambda b, r: (jnp.minimum((r + 1) * rows_per_tile, last_row), b)),
        ]
        x_args = [x2, x2, x2]
    m_ext = n_out + (2 * width if multi_row else 0)
    n_groups = _ffn_row_groups(rows_per_tile, m_ext)
    drow = _const_spec((1, d))
    return pl.pallas_call(
        functools.partial(_ffn_kernel, width=width, multi_row=multi_row, n_groups=n_groups, alpha=alpha),
        name="conv_ffn",
        grid=(bsz, n_r),
        in_specs=x_specs + [
            pl.BlockSpec((1, N_MOD, d), lambda b, r: (b, 0, 0)),
            _resident_spec(p["ffn_up"].shape), _resident_spec(p["ffn_dw"].shape),
            _resident_spec(p["ffn_dw_b"].shape), _resident_spec(p["ffn_down"].shape),
            drow, drow,
        ],
        out_specs=pl.BlockSpec((1, n_out, d), lambda b, r: (b, r, 0)) if batch_major_out else main,
        out_shape=jax.ShapeDtypeStruct((bsz, t, d) if batch_major_out else (t, bsz * d), F32),
        scratch_shapes=[
            pltpu.VMEM((m_ext, d), BF16),
            pltpu.VMEM((m_ext, 2 * fc), F32),
            pltpu.VMEM((m_ext, 2 * fc), F32),
            pltpu.VMEM((n_out, p["ffn_down"].shape[0]), BF16),
        ],
        compiler_params=_cparams("arbitrary", "arbitrary"),
    )(*x_args, mod, p["ffn_up"], p["ffn_dw"], p["ffn_dw_b"], p["ffn_down"], p["ln2_g"], p["ln2_b"])


def _seq_positions(t):
    return 64 if t % 64 == 0 else t


def _ffn_rows_per_tile(rows):
    return 16 if rows % 16 == 0 else rows


def _ffn_row_groups(rows_per_tile, m_ext):
    n = max(rows_per_tile // 8, 1)
    while m_ext % (16 * n) or rows_per_tile % n:
        n -= 1
    return n


def _ffn_chunk(f_dim):
    return 256 if f_dim % 256 == 0 else 128


def _block_diag_dense(w):
    nh, dh, _ = w.shape
    eye = jnp.eye(nh, dtype=w.dtype)
    return jnp.einsum("hij,hg->higj", w, eye).reshape(nh * dh, nh * dh)


def _layer_params(l, w_in, conv_w, conv_b, conv_ln_g, conv_ln_b, rg_conv_w, rg_conv_b, rg_wa, rg_ba, rg_wx,
                  rg_bx, rg_lambda, w_out, ln1_g, ln1_b, ffn_up, ffn_dw, ffn_dw_b, ffn_down, ln2_g, ln2_b):
    row = lambda v: v.reshape(1, -1)
    p = {
        "w_in": w_in[l].astype(BF16),
        "conv_w": conv_w[l], "conv_b": row(conv_b[l]),
        "conv_ln_g": row(conv_ln_g[l]), "conv_ln_b": row(conv_ln_b[l]),
        "rg_conv_w": rg_conv_w[l], "rg_conv_b": row(rg_conv_b[l]),
        "w_out": w_out[l].astype(BF16), "ln1_g": row(ln1_g[l]), "ln1_b": row(ln1_b[l]),
        "ln2_g": row(ln2_g[l]), "ln2_b": row(ln2_b[l]),
    }
    f_dim, d = ffn_down.shape[1:]
    fc = _ffn_chunk(f_dim)
    n_c = 2 * f_dim // fc
    p["ffn_up"] = ffn_up[l].astype(BF16).reshape(d, n_c, fc).transpose(1, 0, 2)
    p["ffn_dw"] = ffn_dw[l].reshape(9, n_c, fc).transpose(1, 0, 2)
    p["ffn_dw_b"] = ffn_dw_b[l].reshape(n_c, 1, fc)
    p["ffn_down"] = ffn_down[l].astype(BF16)
    for k, name in ((0, "f"), (1, "b")):
        p["wax_" + name] = jnp.concatenate(
            [_block_diag_dense(rg_wa[l, k]), _block_diag_dense(rg_wx[l, k])], axis=1).astype(BF16)
        p["ba_" + name] = row(rg_ba[l, k])
        p["bx_" + name] = row(rg_bx[l, k])
        p["lam_" + name] = row(rg_lambda[l, k])
    return p


def _mixer(x2, mod, p, h0_f, h0_b, alpha):
    bsz, _, d = mod.shape
    t = x2.shape[0]
    x3 = x2.reshape(t, bsz, d)
    n_pos = _seq_positions(t)
    u3, rx3, gg3 = _in_proj(x3, mod[:, 0, :], mod[:, 1, :], p["w_in"], n_pos)
    hb3, hb_fin = _bwd_scan(rx3, h0_b, p, n_pos)
    x_new, hf_fin = _mix_out(x3, mod[:, 2, :], u3, rx3, gg3, hb3, h0_f, p, n_pos, alpha)
    return x_new.reshape(t, bsz * d), hf_fin, hb_fin


def kernel(x, c, ctx, c_ctx, w_ada, b_ada, w_in, conv_w, conv_b, conv_ln_g, conv_ln_b, rg_conv_w, rg_conv_b,
           rg_wa, rg_ba, rg_wx, rg_bx, rg_lambda, w_out, ln1_g, ln1_b, ffn_up, ffn_dw, ffn_dw_b, ffn_down,
           ln2_g, ln2_b):
    bsz, seq, d = x.shape
    ctx_len = ctx.shape[1]
    depth = w_in.shape[0]
    alpha = (2.0 * depth) ** 0.25
    cw = rg_lambda.shape[-1]
    assert bsz == SUBLANES, "the sequence kernels put the batch on the 8 sublanes of a vreg"

    pad = (-(bsz + 1)) % 8
    cc = jnp.concatenate([c, c_ctx[None, :], jnp.zeros((pad, d), F32)], axis=0)
    mods = _ada_rows(cc, w_ada, b_ada).reshape(depth, bsz + 1 + pad, N_MOD, d)
    zeros_state = jnp.zeros((bsz, cw), F32)

    xs = jnp.transpose(x, (1, 0, 2)).reshape(seq, bsz * d)
    cs = jnp.transpose(ctx, (1, 0, 2)).reshape(ctx_len, bsz * d)
    rows = seq // LATENT_GRID_W

    for l in range(depth):
        p = _layer_params(l, w_in, conv_w, conv_b, conv_ln_g, conv_ln_b, rg_conv_w, rg_conv_b, rg_wa, rg_ba,
                          rg_wx, rg_bx, rg_lambda, w_out, ln1_g, ln1_b, ffn_up, ffn_dw, ffn_dw_b, ffn_down,
                          ln2_g, ln2_b)
        mod_lat = mods[l, :bsz]
        mod_ctx = jnp.broadcast_to(mods[l, bsz:bsz + 1], (bsz, N_MOD, d))
        cs_mid, hc_f, hc_b = _mixer(cs, mod_ctx, p, zeros_state, zeros_state, alpha)
        xs, _, _ = _mixer(xs, mod_lat, p, hc_f, hc_b, alpha)
        xs = _conv_ffn(xs, mod_lat, p, LATENT_GRID_W, _ffn_rows_per_tile(rows), alpha,
                       batch_major_out=(l == depth - 1))
        if l < depth - 1:
            cs = _conv_ffn(cs_mid, mod_ctx, p, ctx_len, 1, alpha)
    return xs
```

```python
import functools

import jax
import jax.numpy as jnp
from jax import lax
from jax.experimental import pallas as pl
from jax.experimental.pallas import tpu as pltpu

F32 = jnp.float32
BF16 = jnp.bfloat16

LN_EPS = 1e-5
RG_C = 8.0
N_MOD = 6
LATENT_GRID_W = 64
CONV_HALO = 16
VMEM_LIMIT_BYTES = 56 * 1024 * 1024
LANES = 128
SUBLANES = 8


def _cparams(*sem):
    return pltpu.CompilerParams(dimension_semantics=sem, vmem_limit_bytes=VMEM_LIMIT_BYTES)


def _const_spec(shape):
    nd = len(shape)
    return pl.BlockSpec(shape, lambda *_: (0,) * nd)


def _resident_spec(shape):
    nd = len(shape)
    return pl.BlockSpec(shape, lambda *_: (0,) * nd, pipeline_mode=pl.Buffered(1))


def _layer_norm(v, g, b):
    mu = jnp.mean(v, axis=-1, keepdims=True)
    d = v - mu
    var = jnp.mean(d * d, axis=-1, keepdims=True)
    return d * lax.rsqrt(var + LN_EPS) * g + b


def _sigmoid(v):
    return 0.5 + 0.5 * jnp.tanh(0.5 * v)


def _split_bf16(v):
    hi = v.astype(BF16)
    lo = (v - hi.astype(F32)).astype(BF16)
    return hi, lo


def _ada_kernel(c_ref, w_ref, b_ref, o_ref):
    c = c_ref[...]
    s = c * jax.nn.sigmoid(c)
    s_hi, s_lo = _split_bf16(s)
    w_hi, w_lo = _split_bf16(w_ref[0])
    acc = jnp.dot(s_hi, w_hi, preferred_element_type=F32)
    acc += jnp.dot(s_hi, w_lo, preferred_element_type=F32)
    acc += jnp.dot(s_lo, w_hi, preferred_element_type=F32)
    o_ref[0] = acc + b_ref[0]


def _ada_rows(cc, w_ada, b_ada):
    n_layers, d, nc = w_ada.shape
    rows = cc.shape[0]
    tn = 1536
    return pl.pallas_call(
        _ada_kernel,
        name="ada_rows",
        grid=(n_layers, nc // tn),
        in_specs=[
            pl.BlockSpec((rows, d), lambda l, j: (0, 0)),
            pl.BlockSpec((1, d, tn), lambda l, j: (l, 0, j)),
            pl.BlockSpec((1, 1, tn), lambda l, j: (l, 0, j)),
        ],
        out_specs=pl.BlockSpec((1, rows, tn), lambda l, j: (l, 0, j)),
        out_shape=jax.ShapeDtypeStruct((n_layers, rows, nc), F32),
        compiler_params=_cparams("arbitrary", "arbitrary"),
    )(cc, w_ada, b_ada.reshape(n_layers, 1, nc))


def _inproj_kernel(x_ref, shift_ref, scale_ref, w_ref, u_ref, rx_ref, gg_ref):
    n_pos, bsz, d = x_ref.shape
    h = x_ref[...] * (1.0 + scale_ref[...])[None, :, :] + shift_ref[...][None, :, :]
    h = h.reshape(n_pos * bsz, d).astype(BF16)
    z = jnp.dot(h, w_ref[...], preferred_element_type=F32)
    cw = z.shape[1] // 4
    tile = lambda v: v.reshape(n_pos, bsz, cw)
    u_ref[...] = tile(z[:, :cw] * jax.nn.sigmoid(z[:, cw:2 * cw]))
    rx_ref[...] = tile(z[:, 2 * cw:3 * cw])
    gg_ref[...] = tile(jax.nn.gelu(z[:, 3 * cw:]))


def _in_proj(x3, shift, scale, w_in_bf, n_pos):
    t, bsz, d = x3.shape
    cw = w_in_bf.shape[1] // 4
    tile = pl.BlockSpec((n_pos, bsz, cw), lambda i: (i, 0, 0))
    return pl.pallas_call(
        _inproj_kernel,
        name="in_proj",
        grid=(t // n_pos,),
        in_specs=[
            pl.BlockSpec((n_pos, bsz, d), lambda i: (i, 0, 0)),
            _const_spec((bsz, d)), _const_spec((bsz, d)),
            _const_spec(w_in_bf.shape),
        ],
        out_specs=[tile] * 3,
        out_shape=[jax.ShapeDtypeStruct((t, bsz, cw), F32)] * 3,
        compiler_params=_cparams("arbitrary"),
    )(x3, shift, scale, w_in_bf)


def _rows(ref):
    v = ref[...]
    return v.reshape(v.shape[0] * v.shape[1], v.shape[2])


def _short_conv(prev_ref, main_ref, next_ref, w_ref, b_ref, first, last):
    main = _rows(main_ref)
    prev = jnp.where(first, 0.0, _rows(prev_ref))
    nxt = jnp.where(last, 0.0, _rows(next_ref))
    ext = jnp.concatenate([prev, main, nxt], axis=0)
    n = main.shape[0]
    w = w_ref[...]
    acc = b_ref[...] + w[0:1] * ext[0:n]
    for k in range(1, w.shape[0]):
        acc = acc + w[k:k + 1] * ext[SUBLANES * k:SUBLANES * k + n]
    return acc


def _rglru_terms(xr, wax_ref, ba_ref, bx_ref, lam_ref):
    n_blk, bw, _ = wax_ref.shape
    xb = xr.astype(BF16)
    gs = [jnp.dot(xb[:, j * bw:(j + 1) * bw], wax_ref[j], preferred_element_type=F32) for j in range(n_blk)]
    ga = jnp.concatenate([g[:, :bw] for g in gs], axis=1)
    gx = jnp.concatenate([g[:, bw:] for g in gs], axis=1)
    r = _sigmoid(ga + ba_ref[...])
    i = _sigmoid(gx + bx_ref[...])
    nl = -lam_ref[...]
    softplus = jnp.maximum(nl, 0.0) + jnp.log1p(jnp.exp(-jnp.abs(nl)))
    log_a = -RG_C * r * softplus
    a = jnp.exp(log_a)
    th = jnp.tanh(log_a)
    u = jnp.sqrt(-2.0 * th / (1.0 - th)) * (i * xr)
    return a, u


def _scan_positions(a_scr, u_scr, write_h, h_start, n_pos, reverse):
    def body(j, h):
        p = (n_pos - 1 - j) if reverse else j
        rows = pl.ds(pl.multiple_of(p * SUBLANES, SUBLANES), SUBLANES)
        h = a_scr[rows, :] * h + u_scr[rows, :]
        write_h(p, rows, h)
        return h

    return lax.fori_loop(0, n_pos, body, h_start, unroll=4)


def _bwd_scan_kernel(rxp_ref, rxm_ref, rxn_ref, h0_ref, cw_ref, cb_ref, wax_ref, ba_ref, bx_ref, lam_ref,
                     hb_ref, hfin_ref, a_scr, u_scr, carry_scr):
    step = pl.program_id(0)
    n_t = pl.num_programs(0)
    tile = n_t - 1 - step
    n_pos = rxm_ref.shape[0]

    @pl.when(step == 0)
    def _():
        carry_scr[...] = h0_ref[...]

    xr = _short_conv(rxp_ref, rxm_ref, rxn_ref, cw_ref, cb_ref, tile == 0, tile == n_t - 1)
    a, u = _rglru_terms(xr, wax_ref, ba_ref, bx_ref, lam_ref)
    a_scr[...] = a
    u_scr[...] = u

    def write_h(p, rows, h):
        hb_ref[p] = h

    h = _scan_positions(a_scr, u_scr, write_h, carry_scr[...], n_pos, reverse=True)
    carry_scr[...] = h
    hfin_ref[...] = h


def _halo_specs(n_pos, halo_prev, halo_next, t_total, tile_of, tail):
    prev = pl.BlockSpec((halo_prev,) + tail,
                        lambda i: (jnp.maximum(tile_of(i) * (n_pos // halo_prev) - 1, 0), 0, 0))
    nxt = pl.BlockSpec((halo_next,) + tail,
                       lambda i: (jnp.minimum((tile_of(i) + 1) * (n_pos // halo_next), t_total // halo_next - 1),
                                  0, 0))
    return prev, nxt


def _bwd_scan(rx3, h0, p, n_pos):
    t, bsz, cw = rx3.shape
    n_t = t // n_pos
    rev = lambda i: n_t - 1 - i
    prev, nxt = _halo_specs(n_pos, 2, 1, t, rev, (bsz, cw))
    row = _const_spec((1, cw))
    state = _const_spec((bsz, cw))
    return pl.pallas_call(
        _bwd_scan_kernel,
        name="bwd_scan",
        grid=(n_t,),
        in_specs=[
            prev, pl.BlockSpec((n_pos, bsz, cw), lambda i: (rev(i), 0, 0)), nxt, state,
            _const_spec(p["rg_conv_w"].shape), row, _const_spec(p["wax_b"].shape), row, row, row,
        ],
        out_specs=[pl.BlockSpec((n_pos, bsz, cw), lambda i: (rev(i), 0, 0)), state],
        out_shape=[jax.ShapeDtypeStruct((t, bsz, cw), F32), jax.ShapeDtypeStruct((bsz, cw), F32)],
        scratch_shapes=[pltpu.VMEM((n_pos * bsz, cw), F32), pltpu.VMEM((n_pos * bsz, cw), F32),
                        pltpu.VMEM((bsz, cw), F32)],
        compiler_params=_cparams("arbitrary"),
    )(rx3, rx3, rx3, h0, p["rg_conv_w"], p["rg_conv_b"], p["wax_b"], p["ba_b"], p["bx_b"], p["lam_b"])


CONV_STRIP = 64


def _mix_out_kernel(x_ref, g1_ref, up_ref, um_ref, un_ref, rxp_ref, rxm_ref, rxn_ref, gg_ref, hb_ref, h0_ref,
                    cvw_ref, cvb_ref, cvg_ref, cvbeta_ref, cw_ref, cb_ref, wax_ref, ba_ref, bx_ref, lam_ref,
                    wo_ref, lng_ref, lnb_ref, o_ref, hfin_ref, ext_scr, cv_scr, a_scr, u_scr, hf_scr, carry_scr,
                    *, alpha):
    i = pl.program_id(0)
    n_t = pl.num_programs(0)
    first = i == 0
    last = i == n_t - 1
    n_pos, bsz, cw = um_ref.shape
    n_rows = n_pos * bsz
    halo_rows = CONV_HALO * bsz

    @pl.when(first)
    def _():
        carry_scr[...] = h0_ref[...]

    ext_prev = jnp.where(first, 0.0, _rows(up_ref))
    ext_main = _rows(um_ref)
    ext_next = jnp.where(last, 0.0, _rows(un_ref))
    for j in range(cw // LANES):
        ext_scr[j, 0:halo_rows, :] = ext_prev[:, j * LANES:(j + 1) * LANES]
        ext_scr[j, halo_rows:halo_rows + n_rows, :] = ext_main[:, j * LANES:(j + 1) * LANES]
        ext_scr[j, halo_rows + n_rows:, :] = ext_next[:, j * LANES:(j + 1) * LANES]
    n_taps = cvw_ref.shape[0]
    off = (CONV_HALO - (n_taps - 1) // 2) * bsz

    for j in range(cw // LANES):
        lanes = pl.ds(j * LANES, LANES)

        def conv_strip(s, carry, j=j, lanes=lanes):
            base = pl.multiple_of(s * CONV_STRIP, CONV_STRIP)
            parts = [None] * 4
            for k in range(n_taps):
                term = cvw_ref[k:k + 1, lanes] * ext_scr[j, pl.ds(base + off + k * bsz, CONV_STRIP), :]
                parts[k % 4] = term if parts[k % 4] is None else parts[k % 4] + term
            cv_scr[pl.ds(base, CONV_STRIP), lanes] = ((parts[0] + parts[1]) + (parts[2] + parts[3])) + cvb_ref[:, lanes]
            return carry

        lax.fori_loop(0, n_rows // CONV_STRIP, conv_strip, 0)
    v = _layer_norm(cv_scr[...], cvg_ref[...], cvbeta_ref[...])
    y_cv = v * jax.nn.sigmoid(v)

    xr = _short_conv(rxp_ref, rxm_ref, rxn_ref, cw_ref, cb_ref, first, last)
    a, u = _rglru_terms(xr, wax_ref, ba_ref, bx_ref, lam_ref)
    a_scr[...] = a
    u_scr[...] = u

    def write_h(p, rows, h):
        hf_scr[rows, :] = h

    h = _scan_positions(a_scr, u_scr, write_h, carry_scr[...], n_pos, reverse=False)
    carry_scr[...] = h
    hfin_ref[...] = h
    y_rg = (hf_scr[...] + _rows(hb_ref)) * _rows(gg_ref)

    m = jnp.dot(y_cv.astype(BF16), wo_ref[0:cw, :], preferred_element_type=F32)
    m += jnp.dot(y_rg.astype(BF16), wo_ref[cw:, :], preferred_element_type=F32)
    d = m.shape[1]
    gate = g1_ref[...]
    v = alpha * x_ref[...] + gate[None, :, :] * m.reshape(n_pos, bsz, d)
    o_ref[...] = _layer_norm(v, lng_ref[...], lnb_ref[...])


def _mix_out(x3, g1, u3, rx3, gg3, hb3, h0, p, n_pos, alpha):
    t, bsz, d = x3.shape
    cw = rx3.shape[2]
    n_t = t // n_pos
    ident = lambda i: i
    up, un = _halo_specs(n_pos, CONV_HALO, CONV_HALO, t, ident, (bsz, cw))
    rxp, rxn = _halo_specs(n_pos, 2, 1, t, ident, (bsz, cw))
    tile = pl.BlockSpec((n_pos, bsz, cw), lambda i: (i, 0, 0))
    xtile = pl.BlockSpec((n_pos, bsz, d), lambda i: (i, 0, 0))
    state = _const_spec((bsz, cw))
    row = _const_spec((1, cw))
    drow = _const_spec((1, d))
    n_rows = n_pos * bsz
    return pl.pallas_call(
        functools.partial(_mix_out_kernel, alpha=alpha),
        name="mix_out",
        grid=(n_t,),
        in_specs=[
            xtile, _const_spec((bsz, d)),
            up, tile, un, rxp, tile, rxn, tile, tile, state,
            _const_spec(p["conv_w"].shape), row, row, row,
            _const_spec(p["rg_conv_w"].shape), row, _const_spec(p["wax_f"].shape), row, row, row,
            _const_spec(p["w_out"].shape), drow, drow,
        ],
        out_specs=[xtile, state],
        out_shape=[jax.ShapeDtypeStruct((t, bsz, d), F32), jax.ShapeDtypeStruct((bsz, cw), F32)],
        scratch_shapes=[
            pltpu.VMEM((cw // LANES, n_rows + 2 * CONV_HALO * bsz, LANES), F32),
            pltpu.VMEM((n_rows, cw), F32),
            pltpu.VMEM((n_rows, cw), F32), pltpu.VMEM((n_rows, cw), F32),
            pltpu.VMEM((n_rows, cw), F32),
            pltpu.VMEM((bsz, cw), F32),
        ],
        compiler_params=_cparams("arbitrary"),
    )(x3, g1, u3, u3, u3, rx3, rx3, rx3, gg3, hb3, h0,
      p["conv_w"], p["conv_b"], p["conv_ln_g"], p["conv_ln_b"],
      p["rg_conv_w"], p["rg_conv_b"], p["wax_f"], p["ba_f"], p["bx_f"], p["lam_f"],
      p["w_out"], p["ln1_g"], p["ln1_b"])


def _grid_conv_row(z_ref, col0, row0, width, w9, bias, multi_row):
    lanes = pl.ds(col0, LANES)
    if multi_row:
        zt = z_ref[pl.ds(row0, width), lanes]
        zm = z_ref[pl.ds(row0 + width, width), lanes]
        zb = z_ref[pl.ds(row0 + 2 * width, width), lanes]
        v = [w9[dj] * zt + w9[3 + dj] * zm + w9[6 + dj] * zb for dj in range(3)]
    else:
        zm = z_ref[pl.ds(row0, width), lanes]
        v = [w9[3 + dj] * zm for dj in range(3)]
    pos = lax.broadcasted_iota(jnp.int32, (width, LANES), 0)
    left = jnp.where(pos == 0, 0.0, pltpu.roll(v[0], 1, axis=0))
    right = jnp.where(pos == width - 1, 0.0, pltpu.roll(v[2], width - 1, axis=0))
    return (v[1] + bias) + (left + right)


GELU_K0 = (2.0 / 3.141592653589793) ** 0.5
GELU_K1 = GELU_K0 * 0.044715


def _twice_gelu(g):
    return g + g * jnp.tanh(g * (GELU_K0 + GELU_K1 * (g * g)))


def _ffn_kernel(*refs, width, multi_row, n_groups, alpha):
    if multi_row:
        xp_ref, xm_ref, xn_ref = refs[:3]
        refs = refs[3:]
    else:
        xm_ref = refs[0]
        refs = refs[1:]
    mod_ref, wu_ref, dw_ref, db_ref, wd_ref, lng_ref, lnb_ref, o_ref, h_scr, za_scr, zb_scr, y_scr = refs
    n_out = xm_ref.shape[0]
    fc = db_ref.shape[2]
    n_f = wd_ref.shape[0] // fc
    off = width if multi_row else 0

    shift = mod_ref[0, 3:4, :]
    scale = mod_ref[0, 4:5, :]
    h_scr[off:off + n_out] = (xm_ref[...] * (1.0 + scale) + shift).astype(BF16)
    if multi_row:
        r = pl.program_id(1)
        n_r = pl.num_programs(1)
        hp = xp_ref[...] * (1.0 + scale) + shift
        hn = xn_ref[...] * (1.0 + scale) + shift
        h_scr[0:width] = jnp.where(r == 0, 0.0, hp).astype(BF16)
        h_scr[off + n_out:] = jnp.where(r == n_r - 1, 0.0, hn).astype(BF16)

    m_ext = h_scr.shape[0]
    up_rows = m_ext // n_groups
    out_rows = n_out // n_groups

    z_bufs = (za_scr, zb_scr)

    def up_project(f, slot, g):
        z_scr = z_bufs[slot]
        rows = pl.ds(pl.multiple_of(g * up_rows, 16), up_rows)
        h = h_scr[rows, :]
        z_scr[rows, 0:fc] = jnp.dot(h, wu_ref[f], preferred_element_type=F32)
        z_scr[rows, fc:2 * fc] = jnp.dot(h, wu_ref[n_f + f], preferred_element_type=F32)

    def conv_gate_down(f, slot, g):
        z_scr = z_bufs[slot]
        base = pl.multiple_of(g * out_rows, out_rows)
        for cb in range(0, fc, LANES):
            wu9 = [dw_ref[f, k:k + 1, cb:cb + LANES] for k in range(9)]
            wg9 = [dw_ref[n_f + f, k:k + 1, cb:cb + LANES] for k in range(9)]
            bu = db_ref[f, :, cb:cb + LANES]
            bg = db_ref[n_f + f, :, cb:cb + LANES]
            for r0 in range(0, out_rows, width):
                cu = _grid_conv_row(z_scr, cb, base + r0, width, wu9, bu, multi_row)
                cg = _grid_conv_row(z_scr, fc + cb, base + r0, width, wg9, bg, multi_row)
                cols = pl.ds(pl.multiple_of(f * fc, fc) + cb, LANES)
                y_scr[pl.ds(base + r0, width), cols] = (cu * _twice_gelu(cg)).astype(BF16)

    def groups(body):
        lax.fori_loop(0, n_groups, lambda g, c: (body(g), c)[1], 0)

    groups(lambda g: up_project(0, 0, g))

    def stage(f, slot):
        def body(g):
            up_project(f + 1, 1 - slot, g)
            conv_gate_down(f, slot, g)
        groups(body)

    def chunk_pair(j, carry):
        stage(2 * j, 0)
        stage(2 * j + 1, 1)
        return carry

    lax.fori_loop(0, (n_f - 1) // 2, chunk_pair, 0)
    if (n_f - 1) % 2:
        stage(n_f - 2, 0)
    groups(lambda g: conv_gate_down(n_f - 1, (n_f - 1) % 2, g))

    ffn = jnp.dot(y_scr[...], wd_ref[...], preferred_element_type=F32)
    gate = mod_ref[0, 5:6, :]
    out = _layer_norm(alpha * xm_ref[...] + gate * ffn, lng_ref[...], lnb_ref[...])
    o_ref[...] = out.reshape(o_ref.shape)


def _conv_ffn(x2, mod, p, width, rows_per_tile, alpha, batch_major_out=False):
    bsz, _, d = mod.shape
    t = x2.shape[0]
    n_out = rows_per_tile * width
    n_r = t // n_out
    multi_row = t > width
    fc = p["ffn_dw_b"].shape[2]
    last_row = t // width - 1
    main = pl.BlockSpec((n_out, d), lambda b, r: (r, b))
    x_specs = [main]
    x_args = [x2]
    if multi_row:
        x_specs = [
            pl.BlockSpec((width, d), lambda b, r: (jnp.maximum(r * rows_per_tile - 1, 0), b)),
            main,
            pl.BlockSpec((width, d), lambda b, r: (jnp.minimum((r + 1) * rows_per_tile, last_row), b)),
        ]
        x_args = [x2, x2, x2]
    m_ext = n_out + (2 * width if multi_row else 0)
    n_groups = _ffn_row_groups(rows_per_tile, m_ext)
    drow = _const_spec((1, d))
    return pl.pallas_call(
        functools.partial(_ffn_kernel, width=width, multi_row=multi_row, n_groups=n_groups, alpha=alpha),
        name="conv_ffn",
        grid=(bsz, n_r),
        in_specs=x_specs + [
            pl.BlockSpec((1, N_MOD, d), lambda b, r: (b, 0, 0)),
            _resident_spec(p["ffn_up"].shape), _resident_spec(p["ffn_dw"].shape),
            _resident_spec(p["ffn_dw_b"].shape), _resident_spec(p["ffn_down"].shape),
            drow, drow,
        ],
        out_specs=pl.BlockSpec((1, n_out, d), lambda b, r: (b, r, 0)) if batch_major_out else main,
        out_shape=jax.ShapeDtypeStruct((bsz, t, d) if batch_major_out else (t, bsz * d), F32),
        scratch_shapes=[
            pltpu.VMEM((m_ext, d), BF16),
            pltpu.VMEM((m_ext, 2 * fc), F32),
            pltpu.VMEM((m_ext, 2 * fc), F32),
            pltpu.VMEM((n_out, p["ffn_down"].shape[0]), BF16),
        ],
        compiler_params=_cparams("arbitrary", "arbitrary"),
    )(*x_args, mod, p["ffn_up"], p["ffn_dw"], p["ffn_dw_b"], p["ffn_down"], p["ln2_g"], p["ln2_b"])


def _in_proj_positions(t):
    return 128 if t % 128 == 0 else t


def _seq_positions(t):
    return 64 if t % 64 == 0 else t


def _ffn_rows_per_tile(rows):
    return 16 if rows % 16 == 0 else rows


def _ffn_row_groups(rows_per_tile, m_ext):
    n = max(rows_per_tile // 16, 1)
    while m_ext % (16 * n) or rows_per_tile % n:
        n -= 1
    return n


def _ffn_chunk(f_dim):
    return 256 if f_dim % 256 == 0 else 128


GATE_BLOCK = 256


def _block_diag_blocks(w):
    nh, dh, _ = w.shape
    per = GATE_BLOCK // dh
    eye = jnp.eye(per, dtype=w.dtype)
    wb = w.reshape(nh // per, per, dh, dh)
    return jnp.einsum("bhij,hg->bhigj", wb, eye).reshape(nh // per, per * dh, per * dh)


def _layer_params(l, w_in, conv_w, conv_b, conv_ln_g, conv_ln_b, rg_conv_w, rg_conv_b, rg_wa, rg_ba, rg_wx,
                  rg_bx, rg_lambda, w_out, ln1_g, ln1_b, ffn_up, ffn_dw, ffn_dw_b, ffn_down, ln2_g, ln2_b):
    row = lambda v: v.reshape(1, -1)
    p = {
        "w_in": w_in[l].astype(BF16),
        "conv_w": conv_w[l], "conv_b": row(conv_b[l]),
        "conv_ln_g": row(conv_ln_g[l]), "conv_ln_b": row(conv_ln_b[l]),
        "rg_conv_w": rg_conv_w[l], "rg_conv_b": row(rg_conv_b[l]),
        "w_out": w_out[l].astype(BF16), "ln1_g": row(ln1_g[l]), "ln1_b": row(ln1_b[l]),
        "ln2_g": row(ln2_g[l]), "ln2_b": row(ln2_b[l]),
    }
    f_dim, d = ffn_down.shape[1:]
    fc = _ffn_chunk(f_dim)
    n_c = 2 * f_dim // fc
    p["ffn_up"] = ffn_up[l].astype(BF16).reshape(d, n_c, fc).transpose(1, 0, 2)
    half_u = jnp.where(jnp.arange(n_c) < n_c // 2, 0.5, 1.0).astype(F32)[:, None, None]
    p["ffn_dw"] = ffn_dw[l].reshape(9, n_c, fc).transpose(1, 0, 2) * half_u
    p["ffn_dw_b"] = ffn_dw_b[l].reshape(n_c, 1, fc) * half_u
    p["ffn_down"] = ffn_down[l].astype(BF16)
    for k, name in ((0, "f"), (1, "b")):
        p["wax_" + name] = jnp.concatenate(
            [_block_diag_blocks(rg_wa[l, k]), _block_diag_blocks(rg_wx[l, k])], axis=2).astype(BF16)
        p["ba_" + name] = row(rg_ba[l, k])
        p["bx_" + name] = row(rg_bx[l, k])
        p["lam_" + name] = row(rg_lambda[l, k])
    return p


def _mixer(x2, mod, p, h0_f, h0_b, alpha):
    bsz, _, d = mod.shape
    t = x2.shape[0]
    x3 = x2.reshape(t, bsz, d)
    n_pos = _seq_positions(t)
    u3, rx3, gg3 = _in_proj(x3, mod[:, 0, :], mod[:, 1, :], p["w_in"], _in_proj_positions(t))
    hb3, hb_fin = _bwd_scan(rx3, h0_b, p, n_pos)
    x_new, hf_fin = _mix_out(x3, mod[:, 2, :], u3, rx3, gg3, hb3, h0_f, p, n_pos, alpha)
    return x_new.reshape(t, bsz * d), hf_fin, hb_fin


def kernel(x, c, ctx, c_ctx, w_ada, b_ada, w_in, conv_w, conv_b, conv_ln_g, conv_ln_b, rg_conv_w, rg_conv_b,
           rg_wa, rg_ba, rg_wx, rg_bx, rg_lambda, w_out, ln1_g, ln1_b, ffn_up, ffn_dw, ffn_dw_b, ffn_down,
           ln2_g, ln2_b):
    bsz, seq, d = x.shape
    ctx_len = ctx.shape[1]
    depth = w_in.shape[0]
    alpha = (2.0 * depth) ** 0.25
    cw = rg_lambda.shape[-1]
    assert bsz == SUBLANES, "the sequence kernels put the batch on the 8 sublanes of a vreg"

    pad = (-(bsz + 1)) % 8
    cc = jnp.concatenate([c, c_ctx[None, :], jnp.zeros((pad, d), F32)], axis=0)
    mods = _ada_rows(cc, w_ada, b_ada).reshape(depth, bsz + 1 + pad, N_MOD, d)
    zeros_state = jnp.zeros((bsz, cw), F32)

    xs = jnp.transpose(x, (1, 0, 2)).reshape(seq, bsz * d)
    cs = jnp.transpose(ctx, (1, 0, 2)).reshape(ctx_len, bsz * d)
    rows = seq // LATENT_GRID_W

    for l in range(depth):
        p = _layer_params(l, w_in, conv_w, conv_b, conv_ln_g, conv_ln_b, rg_conv_w, rg_conv_b, rg_wa, rg_ba,
                          rg_wx, rg_bx, rg_lambda, w_out, ln1_g, ln1_b, ffn_up, ffn_dw, ffn_dw_b, ffn_down,
                          ln2_g, ln2_b)
        mod_lat = mods[l, :bsz]
        mod_ctx = jnp.broadcast_to(mods[l, bsz:bsz + 1], (bsz, N_MOD, d))
        cs_mid, hc_f, hc_b = _mixer(cs, mod_ctx, p, zeros_state, zeros_state, alpha)
        xs, _, _ = _mixer(xs, mod_lat, p, hc_f, hc_b, alpha)
        xs = _conv_ffn(xs, mod_lat, p, LATENT_GRID_W, _ffn_rows_per_tile(rows), alpha,
                       batch_major_out=(l == depth - 1))
        if l < depth - 1:
            cs = _conv_ffn(cs_mid, mod_ctx, p, ctx_len, 1, alpha)
    return xs
```

```python
import functools

import jax
import jax.numpy as jnp
from jax import lax
from jax.experimental import pallas as pl
from jax.experimental.pallas import tpu as pltpu

F32 = jnp.float32
BF16 = jnp.bfloat16

LN_EPS = 1e-5
RG_C = 8.0
N_MOD = 6
LATENT_GRID_W = 64
CONV_HALO = 16
VMEM_LIMIT_BYTES = 56 * 1024 * 1024
LANES = 128
SUBLANES = 8


def _cparams(*sem):
    return pltpu.CompilerParams(dimension_semantics=sem, vmem_limit_bytes=VMEM_LIMIT_BYTES)


def _const_spec(shape):
    nd = len(shape)
    return pl.BlockSpec(shape, lambda *_: (0,) * nd)


def _resident_spec(shape):
    nd = len(shape)
    return pl.BlockSpec(shape, lambda *_: (0,) * nd, pipeline_mode=pl.Buffered(1))


def _layer_norm(v, g, b):
    mu = jnp.mean(v, axis=-1, keepdims=True)
    d = v - mu
    var = jnp.mean(d * d, axis=-1, keepdims=True)
    return d * lax.rsqrt(var + LN_EPS) * g + b


def _sigmoid(v):
    return 0.5 + 0.5 * jnp.tanh(0.5 * v)


def _split_bf16(v):
    hi = v.astype(BF16)
    lo = (v - hi.astype(F32)).astype(BF16)
    return hi, lo


def _ada_kernel(c_ref, w_ref, b_ref, o_ref):
    c = c_ref[...]
    s = c * jax.nn.sigmoid(c)
    s_hi, s_lo = _split_bf16(s)
    w_hi, w_lo = _split_bf16(w_ref[0])
    acc = jnp.dot(s_hi, w_hi, preferred_element_type=F32)
    acc += jnp.dot(s_hi, w_lo, preferred_element_type=F32)
    acc += jnp.dot(s_lo, w_hi, preferred_element_type=F32)
    o_ref[0] = acc + b_ref[0]


def _ada_rows(cc, w_ada, b_ada):
    n_layers, d, nc = w_ada.shape
    rows = cc.shape[0]
    tn = 1536
    return pl.pallas_call(
        _ada_kernel,
        name="ada_rows",
        grid=(n_layers, nc // tn),
        in_specs=[
            pl.BlockSpec((rows, d), lambda l, j: (0, 0)),
            pl.BlockSpec((1, d, tn), lambda l, j: (l, 0, j)),
            pl.BlockSpec((1, 1, tn), lambda l, j: (l, 0, j)),
        ],
        out_specs=pl.BlockSpec((1, rows, tn), lambda l, j: (l, 0, j)),
        out_shape=jax.ShapeDtypeStruct((n_layers, rows, nc), F32),
        compiler_params=_cparams("arbitrary", "arbitrary"),
    )(cc, w_ada, b_ada.reshape(n_layers, 1, nc))


def _inproj_kernel(x_ref, shift_ref, scale_ref, w_ref, u_ref, rx_ref, gg_ref):
    n_pos, bsz, d = x_ref.shape
    h = x_ref[...] * (1.0 + scale_ref[...])[None, :, :] + shift_ref[...][None, :, :]
    h = h.reshape(n_pos * bsz, d).astype(BF16)
    z = jnp.dot(h, w_ref[...], preferred_element_type=F32)
    cw = z.shape[1] // 4
    tile = lambda v: v.reshape(n_pos, bsz, cw)
    u_ref[...] = tile(z[:, :cw] * jax.nn.sigmoid(z[:, cw:2 * cw]))
    rx_ref[...] = tile(z[:, 2 * cw:3 * cw])
    gg_ref[...] = tile(jax.nn.gelu(z[:, 3 * cw:]))


def _in_proj(x3, shift, scale, w_in_bf, n_pos):
    t, bsz, d = x3.shape
    cw = w_in_bf.shape[1] // 4
    tile = pl.BlockSpec((n_pos, bsz, cw), lambda i: (i, 0, 0))
    return pl.pallas_call(
        _inproj_kernel,
        name="in_proj",
        grid=(t // n_pos,),
        in_specs=[
            pl.BlockSpec((n_pos, bsz, d), lambda i: (i, 0, 0)),
            _const_spec((bsz, d)), _const_spec((bsz, d)),
            _const_spec(w_in_bf.shape),
        ],
        out_specs=[tile] * 3,
        out_shape=[jax.ShapeDtypeStruct((t, bsz, cw), F32)] * 3,
        compiler_params=_cparams("arbitrary"),
    )(x3, shift, scale, w_in_bf)


def _rows(ref):
    v = ref[...]
    return v.reshape(v.shape[0] * v.shape[1], v.shape[2])


def _short_conv(prev_ref, main_ref, next_ref, w_ref, b_ref, first, last):
    main = _rows(main_ref)
    prev = jnp.where(first, 0.0, _rows(prev_ref))
    nxt = jnp.where(last, 0.0, _rows(next_ref))
    ext = jnp.concatenate([prev, main, nxt], axis=0)
    n = main.shape[0]
    w = w_ref[...]
    acc = b_ref[...] + w[0:1] * ext[0:n]
    for k in range(1, w.shape[0]):
        acc = acc + w[k:k + 1] * ext[SUBLANES * k:SUBLANES * k + n]
    return acc


def _rglru_terms(xr, wax_ref, ba_ref, bx_ref, lam_ref):
    n_blk, bw, _ = wax_ref.shape
    xb = xr.astype(BF16)
    gs = [jnp.dot(xb[:, j * bw:(j + 1) * bw], wax_ref[j], preferred_element_type=F32) for j in range(n_blk)]
    ga = jnp.concatenate([g[:, :bw] for g in gs], axis=1)
    gx = jnp.concatenate([g[:, bw:] for g in gs], axis=1)
    r = _sigmoid(ga + ba_ref[...])
    i = _sigmoid(gx + bx_ref[...])
    nl = -lam_ref[...]
    softplus = jnp.maximum(nl, 0.0) + jnp.log1p(jnp.exp(-jnp.abs(nl)))
    log_a = -RG_C * r * softplus
    a = jnp.exp(log_a)
    th = jnp.tanh(log_a)
    u = jnp.sqrt(-2.0 * th / (1.0 - th)) * (i * xr)
    return a, u


def _scan_positions(a_scr, u_scr, write_h, h_start, n_pos, reverse):
    def body(j, h):
        p = (n_pos - 1 - j) if reverse else j
        rows = pl.ds(pl.multiple_of(p * SUBLANES, SUBLANES), SUBLANES)
        h = a_scr[rows, :] * h + u_scr[rows, :]
        write_h(p, rows, h)
        return h

    return lax.fori_loop(0, n_pos, body, h_start, unroll=4)


def _bwd_scan_kernel(rxp_ref, rxm_ref, rxn_ref, h0_ref, cw_ref, cb_ref, wax_ref, ba_ref, bx_ref, lam_ref,
                     hb_ref, hfin_ref, a_scr, u_scr, carry_scr):
    step = pl.program_id(0)
    n_t = pl.num_programs(0)
    tile = n_t - 1 - step
    n_pos = rxm_ref.shape[0]

    @pl.when(step == 0)
    def _():
        carry_scr[...] = h0_ref[...]

    xr = _short_conv(rxp_ref, rxm_ref, rxn_ref, cw_ref, cb_ref, tile == 0, tile == n_t - 1)
    a, u = _rglru_terms(xr, wax_ref, ba_ref, bx_ref, lam_ref)
    a_scr[...] = a
    u_scr[...] = u

    def write_h(p, rows, h):
        hb_ref[p] = h

    h = _scan_positions(a_scr, u_scr, write_h, carry_scr[...], n_pos, reverse=True)
    carry_scr[...] = h
    hfin_ref[...] = h


def _halo_specs(n_pos, halo_prev, halo_next, t_total, tile_of, tail):
    prev = pl.BlockSpec((halo_prev,) + tail,
                        lambda i: (jnp.maximum(tile_of(i) * (n_pos // halo_prev) - 1, 0), 0, 0))
    nxt = pl.BlockSpec((halo_next,) + tail,
                       lambda i: (jnp.minimum((tile_of(i) + 1) * (n_pos // halo_next), t_total // halo_next - 1),
                                  0, 0))
    return prev, nxt


def _bwd_scan(rx3, h0, p, n_pos):
    t, bsz, cw = rx3.shape
    n_t = t // n_pos
    rev = lambda i: n_t - 1 - i
    prev, nxt = _halo_specs(n_pos, 2, 1, t, rev, (bsz, cw))
    row = _const_spec((1, cw))
    state = _const_spec((bsz, cw))
    return pl.pallas_call(
        _bwd_scan_kernel,
        name="bwd_scan",
        grid=(n_t,),
        in_specs=[
            prev, pl.BlockSpec((n_pos, bsz, cw), lambda i: (rev(i), 0, 0)), nxt, state,
            _const_spec(p["rg_conv_w"].shape), row, _const_spec(p["wax_b"].shape), row, row, row,
        ],
        out_specs=[pl.BlockSpec((n_pos, bsz, cw), lambda i: (rev(i), 0, 0)), state],
        out_shape=[jax.ShapeDtypeStruct((t, bsz, cw), F32), jax.ShapeDtypeStruct((bsz, cw), F32)],
        scratch_shapes=[pltpu.VMEM((n_pos * bsz, cw), F32), pltpu.VMEM((n_pos * bsz, cw), F32),
                        pltpu.VMEM((bsz, cw), F32)],
        compiler_params=_cparams("arbitrary"),
    )(rx3, rx3, rx3, h0, p["rg_conv_w"], p["rg_conv_b"], p["wax_b"], p["ba_b"], p["bx_b"], p["lam_b"])


CONV_STRIP = 64


def _mix_out_kernel(x_ref, g1_ref, up_ref, um_ref, un_ref, rxp_ref, rxm_ref, rxn_ref, gg_ref, hb_ref, h0_ref,
                    cvw_ref, cvb_ref, cvg_ref, cvbeta_ref, cw_ref, cb_ref, wax_ref, ba_ref, bx_ref, lam_ref,
                    wo_ref, lng_ref, lnb_ref, o_ref, hfin_ref, ext_scr, cv_scr, a_scr, u_scr, hf_scr, carry_scr,
                    *, alpha):
    i = pl.program_id(0)
    n_t = pl.num_programs(0)
    first = i == 0
    last = i == n_t - 1
    n_pos, bsz, cw = um_ref.shape
    n_rows = n_pos * bsz
    halo_rows = CONV_HALO * bsz

    @pl.when(first)
    def _():
        carry_scr[...] = h0_ref[...]

    ext_prev = jnp.where(first, 0.0, _rows(up_ref))
    ext_main = _rows(um_ref)
    ext_next = jnp.where(last, 0.0, _rows(un_ref))
    for j in range(cw // LANES):
        ext_scr[j, 0:halo_rows, :] = ext_prev[:, j * LANES:(j + 1) * LANES]
        ext_scr[j, halo_rows:halo_rows + n_rows, :] = ext_main[:, j * LANES:(j + 1) * LANES]
        ext_scr[j, halo_rows + n_rows:, :] = ext_next[:, j * LANES:(j + 1) * LANES]
    n_taps = cvw_ref.shape[0]
    off = (CONV_HALO - (n_taps - 1) // 2) * bsz

    for j in range(cw // LANES):
        lanes = pl.ds(j * LANES, LANES)

        def conv_strip(s, carry, j=j, lanes=lanes):
            base = pl.multiple_of(s * CONV_STRIP, CONV_STRIP)
            parts = [None] * 4
            for k in range(n_taps):
                term = cvw_ref[k:k + 1, lanes] * ext_scr[j, pl.ds(base + off + k * bsz, CONV_STRIP), :]
                parts[k % 4] = term if parts[k % 4] is None else parts[k % 4] + term
            cv_scr[pl.ds(base, CONV_STRIP), lanes] = ((parts[0] + parts[1]) + (parts[2] + parts[3])) + cvb_ref[:, lanes]
            return carry

        lax.fori_loop(0, n_rows // CONV_STRIP, conv_strip, 0)
    v = _layer_norm(cv_scr[...], cvg_ref[...], cvbeta_ref[...])
    y_cv = v * jax.nn.sigmoid(v)

    xr = _short_conv(rxp_ref, rxm_ref, rxn_ref, cw_ref, cb_ref, first, last)
    a, u = _rglru_terms(xr, wax_ref, ba_ref, bx_ref, lam_ref)
    a_scr[...] = a
    u_scr[...] = u

    def write_h(p, rows, h):
        hf_scr[rows, :] = h

    h = _scan_positions(a_scr, u_scr, write_h, carry_scr[...], n_pos, reverse=False)
    carry_scr[...] = h
    hfin_ref[...] = h
    y_rg = (hf_scr[...] + _rows(hb_ref)) * _rows(gg_ref)

    m = jnp.dot(y_cv.astype(BF16), wo_ref[0:cw, :], preferred_element_type=F32)
    m += jnp.dot(y_rg.astype(BF16), wo_ref[cw:, :], preferred_element_type=F32)
    d = m.shape[1]
    gate = g1_ref[...]
    v = alpha * x_ref[...] + gate[None, :, :] * m.reshape(n_pos, bsz, d)
    o_ref[...] = _layer_norm(v, lng_ref[...], lnb_ref[...])


def _mix_out(x3, g1, u3, rx3, gg3, hb3, h0, p, n_pos, alpha):
    t, bsz, d = x3.shape
    cw = rx3.shape[2]
    n_t = t // n_pos
    ident = lambda i: i
    up, un = _halo_specs(n_pos, CONV_HALO, CONV_HALO, t, ident, (bsz, cw))
    rxp, rxn = _halo_specs(n_pos, 2, 1, t, ident, (bsz, cw))
    tile = pl.BlockSpec((n_pos, bsz, cw), lambda i: (i, 0, 0))
    xtile = pl.BlockSpec((n_pos, bsz, d), lambda i: (i, 0, 0))
    state = _const_spec((bsz, cw))
    row = _const_spec((1, cw))
    drow = _const_spec((1, d))
    n_rows = n_pos * bsz
    return pl.pallas_call(
        functools.partial(_mix_out_kernel, alpha=alpha),
        name="mix_out",
        grid=(n_t,),
        in_specs=[
            xtile, _const_spec((bsz, d)),
            up, tile, un, rxp, tile, rxn, tile, tile, state,
            _const_spec(p["conv_w"].shape), row, row, row,
            _const_spec(p["rg_conv_w"].shape), row, _const_spec(p["wax_f"].shape), row, row, row,
            _const_spec(p["w_out"].shape), drow, drow,
        ],
        out_specs=[xtile, state],
        out_shape=[jax.ShapeDtypeStruct((t, bsz, d), F32), jax.ShapeDtypeStruct((bsz, cw), F32)],
        scratch_shapes=[
            pltpu.VMEM((cw // LANES, n_rows + 2 * CONV_HALO * bsz, LANES), F32),
            pltpu.VMEM((n_rows, cw), F32),
            pltpu.VMEM((n_rows, cw), F32), pltpu.VMEM((n_rows, cw), F32),
            pltpu.VMEM((n_rows, cw), F32),
            pltpu.VMEM((bsz, cw), F32),
        ],
        compiler_params=_cparams("arbitrary"),
    )(x3, g1, u3, u3, u3, rx3, rx3, rx3, gg3, hb3, h0,
      p["conv_w"], p["conv_b"], p["conv_ln_g"], p["conv_ln_b"],
      p["rg_conv_w"], p["rg_conv_b"], p["wax_f"], p["ba_f"], p["bx_f"], p["lam_f"],
      p["w_out"], p["ln1_g"], p["ln1_b"])


def _grid_conv_row(z_ref, col0, row0, width, w9, bias, multi_row):
    lanes = pl.ds(col0, LANES)
    if multi_row:
        zt = z_ref[pl.ds(row0, width), lanes]
        zm = z_ref[pl.ds(row0 + width, width), lanes]
        zb = z_ref[pl.ds(row0 + 2 * width, width), lanes]
        v = [(w9[dj] * zt + w9[3 + dj] * zm + w9[6 + dj] * zb).astype(F32) for dj in range(3)]
    else:
        zm = z_ref[pl.ds(row0, width), lanes]
        v = [(w9[3 + dj] * zm).astype(F32) for dj in range(3)]
    pos = lax.broadcasted_iota(jnp.int32, (width, LANES), 0)
    left = jnp.where(pos == 0, 0.0, pltpu.roll(v[0], 1, axis=0))
    right = jnp.where(pos == width - 1, 0.0, pltpu.roll(v[2], width - 1, axis=0))
    return (v[1] + bias) + (left + right)


GELU_K0 = (2.0 / 3.141592653589793) ** 0.5
GELU_K1 = GELU_K0 * 0.044715


def _twice_gelu(g):
    return g + g * jnp.tanh(g * (GELU_K0 + GELU_K1 * (g * g)))


def _ffn_kernel(*refs, width, multi_row, n_groups, alpha):
    if multi_row:
        xp_ref, xm_ref, xn_ref = refs[:3]
        refs = refs[3:]
    else:
        xm_ref = refs[0]
        refs = refs[1:]
    mod_ref, wu_ref, dw_ref, db_ref, wd_ref, lng_ref, lnb_ref, o_ref, h_scr, za_scr, zb_scr, y_scr = refs
    n_out = xm_ref.shape[0]
    fc = db_ref.shape[2]
    n_f = wd_ref.shape[0] // fc
    off = width if multi_row else 0

    shift = mod_ref[0, 3:4, :]
    scale = mod_ref[0, 4:5, :]
    h_scr[off:off + n_out] = (xm_ref[...] * (1.0 + scale) + shift).astype(BF16)
    if multi_row:
        r = pl.program_id(1)
        n_r = pl.num_programs(1)
        hp = xp_ref[...] * (1.0 + scale) + shift
        hn = xn_ref[...] * (1.0 + scale) + shift
        h_scr[0:width] = jnp.where(r == 0, 0.0, hp).astype(BF16)
        h_scr[off + n_out:] = jnp.where(r == n_r - 1, 0.0, hn).astype(BF16)

    m_ext = h_scr.shape[0]
    up_rows = m_ext // n_groups
    out_rows = n_out // n_groups

    z_bufs = (za_scr, zb_scr)

    def up_project(f, slot, g):
        z_scr = z_bufs[slot]
        rows = pl.ds(pl.multiple_of(g * up_rows, 16), up_rows)
        h = h_scr[rows, :]
        z_scr[rows, 0:fc] = jnp.dot(h, wu_ref[f], preferred_element_type=F32).astype(BF16)
        z_scr[rows, fc:2 * fc] = jnp.dot(h, wu_ref[n_f + f], preferred_element_type=F32).astype(BF16)

    def conv_gate_down(f, slot, g):
        z_scr = z_bufs[slot]
        base = pl.multiple_of(g * out_rows, out_rows)
        for cb in range(0, fc, LANES):
            wu9 = [dw_ref[f, k:k + 1, cb:cb + LANES].astype(BF16) for k in range(9)]
            wg9 = [dw_ref[n_f + f, k:k + 1, cb:cb + LANES].astype(BF16) for k in range(9)]
            bu = db_ref[f, :, cb:cb + LANES]
            bg = db_ref[n_f + f, :, cb:cb + LANES]
            for r0 in range(0, out_rows, width):
                cu = _grid_conv_row(z_scr, cb, base + r0, width, wu9, bu, multi_row)
                cg = _grid_conv_row(z_scr, fc + cb, base + r0, width, wg9, bg, multi_row)
                cols = pl.ds(pl.multiple_of(f * fc, fc) + cb, LANES)
                y_scr[pl.ds(base + r0, width), cols] = (cu * _twice_gelu(cg)).astype(BF16)

    def groups(body):
        lax.fori_loop(0, n_groups, lambda g, c: (body(g), c)[1], 0)

    groups(lambda g: up_project(0, 0, g))

    def stage(f, slot):
        def body(g):
            up_project(f + 1, 1 - slot, g)
            conv_gate_down(f, slot, g)
        groups(body)

    def chunk_pair(j, carry):
        stage(2 * j, 0)
        stage(2 * j + 1, 1)
        return carry

    lax.fori_loop(0, (n_f - 1) // 2, chunk_pair, 0)
    if (n_f - 1) % 2:
        stage(n_f - 2, 0)
    groups(lambda g: conv_gate_down(n_f - 1, (n_f - 1) % 2, g))

    ffn = jnp.dot(y_scr[...], wd_ref[...], preferred_element_type=F32)
    gate = mod_ref[0, 5:6, :]
    out = _layer_norm(alpha * xm_ref[...] + gate * ffn, lng_ref[...], lnb_ref[...])
    o_ref[...] = out.reshape(o_ref.shape)


def _conv_ffn(x2, mod, p, width, rows_per_tile, alpha, batch_major_out=False):
    bsz, _, d = mod.shape
    t = x2.shape[0]
    n_out = rows_per_tile * width
    n_r = t // n_out
    multi_row = t > width
    fc = p["ffn_dw_b"].shape[2]
    last_row = t // width - 1
    main = pl.BlockSpec((n_out, d), lambda b, r: (r, b))
    x_specs = [main]
    x_args = [x2]
    if multi_row:
        x_specs = [
            pl.BlockSpec((width, d), lambda b, r: (jnp.maximum(r * rows_per_tile - 1, 0), b)),
            main,
            pl.BlockSpec((width, d), lambda b, r: (jnp.minimum((r + 1) * rows_per_tile, last_row), b)),
        ]
        x_args = [x2, x2, x2]
    m_ext = n_out + (2 * width if multi_row else 0)
    n_groups = _ffn_row_groups(rows_per_tile, m_ext)
    drow = _const_spec((1, d))
    return pl.pallas_call(
        functools.partial(_ffn_kernel, width=width, multi_row=multi_row, n_groups=n_groups, alpha=alpha),
        name="conv_ffn",
        grid=(bsz, n_r),
        in_specs=x_specs + [
            pl.BlockSpec((1, N_MOD, d), lambda b, r: (b, 0, 0)),
            _resident_spec(p["ffn_up"].shape), _resident_spec(p["ffn_dw"].shape),
            _resident_spec(p["ffn_dw_b"].shape), _resident_spec(p["ffn_down"].shape),
            drow, drow,
        ],
        out_specs=pl.BlockSpec((1, n_out, d), lambda b, r: (b, r, 0)) if batch_major_out else main,
        out_shape=jax.ShapeDtypeStruct((bsz, t, d) if batch_major_out else (t, bsz * d), F32),
        scratch_shapes=[
            pltpu.VMEM((m_ext, d), BF16),
            pltpu.VMEM((m_ext, 2 * fc), BF16),
            pltpu.VMEM((m_ext, 2 * fc), BF16),
            pltpu.VMEM((n_out, p["ffn_down"].shape[0]), BF16),
        ],
        compiler_params=_cparams("arbitrary", "arbitrary"),
    )(*x_args, mod, p["ffn_up"], p["ffn_dw"], p["ffn_dw_b"], p["ffn_down"], p["ln2_g"], p["ln2_b"])


def _in_proj_positions(t):
    return 128 if t % 128 == 0 else t


def _seq_positions(t):
    return 64 if t % 64 == 0 else t


def _ffn_rows_per_tile(rows):
    return 16 if rows % 16 == 0 else rows


def _ffn_row_groups(rows_per_tile, m_ext):
    n = max(rows_per_tile // 16, 1)
    while m_ext % (16 * n) or rows_per_tile % n:
        n -= 1
    return n


def _ffn_chunk(f_dim):
    return 256 if f_dim % 256 == 0 else 128


GATE_BLOCK = 256


def _block_diag_blocks(w):
    nh, dh, _ = w.shape
    per = GATE_BLOCK // dh
    eye = jnp.eye(per, dtype=w.dtype)
    wb = w.reshape(nh // per, per, dh, dh)
    return jnp.einsum("bhij,hg->bhigj", wb, eye).reshape(nh // per, per * dh, per * dh)


def _layer_params(l, w_in, conv_w, conv_b, conv_ln_g, conv_ln_b, rg_conv_w, rg_conv_b, rg_wa, rg_ba, rg_wx,
                  rg_bx, rg_lambda, w_out, ln1_g, ln1_b, ffn_up, ffn_dw, ffn_dw_b, ffn_down, ln2_g, ln2_b):
    row = lambda v: v.reshape(1, -1)
    p = {
        "w_in": w_in[l].astype(BF16),
        "conv_w": conv_w[l], "conv_b": row(conv_b[l]),
        "conv_ln_g": row(conv_ln_g[l]), "conv_ln_b": row(conv_ln_b[l]),
        "rg_conv_w": rg_conv_w[l], "rg_conv_b": row(rg_conv_b[l]),
        "w_out": w_out[l].astype(BF16), "ln1_g": row(ln1_g[l]), "ln1_b": row(ln1_b[l]),
        "ln2_g": row(ln2_g[l]), "ln2_b": row(ln2_b[l]),
    }
    f_dim, d = ffn_down.shape[1:]
    fc = _ffn_chunk(f_dim)
    n_c = 2 * f_dim // fc
    p["ffn_up"] = ffn_up[l].astype(BF16).reshape(d, n_c, fc).transpose(1, 0, 2)
    half_u = jnp.where(jnp.arange(n_c) < n_c // 2, 0.5, 1.0).astype(F32)[:, None, None]
    p["ffn_dw"] = ffn_dw[l].reshape(9, n_c, fc).transpose(1, 0, 2) * half_u
    p["ffn_dw_b"] = ffn_dw_b[l].reshape(n_c, 1, fc) * half_u
    p["ffn_down"] = ffn_down[l].astype(BF16)
    for k, name in ((0, "f"), (1, "b")):
        p["wax_" + name] = jnp.concatenate(
            [_block_diag_blocks(rg_wa[l, k]), _block_diag_blocks(rg_wx[l, k])], axis=2).astype(BF16)
        p["ba_" + name] = row(rg_ba[l, k])
        p["bx_" + name] = row(rg_bx[l, k])
        p["lam_" + name] = row(rg_lambda[l, k])
    return p


def _mixer(x2, mod, p, h0_f, h0_b, alpha):
    bsz, _, d = mod.shape
    t = x2.shape[0]
    x3 = x2.reshape(t, bsz, d)
    n_pos = _seq_positions(t)
    u3, rx3, gg3 = _in_proj(x3, mod[:, 0, :], mod[:, 1, :], p["w_in"], _in_proj_positions(t))
    hb3, hb_fin = _bwd_scan(rx3, h0_b, p, n_pos)
    x_new, hf_fin = _mix_out(x3, mod[:, 2, :], u3, rx3, gg3, hb3, h0_f, p, n_pos, alpha)
    return x_new.reshape(t, bsz * d), hf_fin, hb_fin


def kernel(x, c, ctx, c_ctx, w_ada, b_ada, w_in, conv_w, conv_b, conv_ln_g, conv_ln_b, rg_conv_w, rg_conv_b,
           rg_wa, rg_ba, rg_wx, rg_bx, rg_lambda, w_out, ln1_g, ln1_b, ffn_up, ffn_dw, ffn_dw_b, ffn_down,
           ln2_g, ln2_b):
    bsz, seq, d = x.shape
    ctx_len = ctx.shape[1]
    depth = w_in.shape[0]
    alpha = (2.0 * depth) ** 0.25
    cw = rg_lambda.shape[-1]
    assert bsz == SUBLANES, "the sequence kernels put the batch on the 8 sublanes of a vreg"

    pad = (-(bsz + 1)) % 8
    cc = jnp.concatenate([c, c_ctx[None, :], jnp.zeros((pad, d), F32)], axis=0)
    mods = _ada_rows(cc, w_ada, b_ada).reshape(depth, bsz + 1 + pad, N_MOD, d)
    zeros_state = jnp.zeros((bsz, cw), F32)

    xs = jnp.transpose(x, (1, 0, 2)).reshape(seq, bsz * d)
    cs = jnp.transpose(ctx, (1, 0, 2)).reshape(ctx_len, bsz * d)
    rows = seq // LATENT_GRID_W

    for l in range(depth):
        p = _layer_params(l, w_in, conv_w, conv_b, conv_ln_g, conv_ln_b, rg_conv_w, rg_conv_b, rg_wa, rg_ba,
                          rg_wx, rg_bx, rg_lambda, w_out, ln1_g, ln1_b, ffn_up, ffn_dw, ffn_dw_b, ffn_down,
                          ln2_g, ln2_b)
        mod_lat = mods[l, :bsz]
        mod_ctx = jnp.broadcast_to(mods[l, bsz:bsz + 1], (bsz, N_MOD, d))
        cs_mid, hc_f, hc_b = _mixer(cs, mod_ctx, p, zeros_state, zeros_state, alpha)
        xs, _, _ = _mixer(xs, mod_lat, p, hc_f, hc_b, alpha)
        xs = _conv_ffn(xs, mod_lat, p, LATENT_GRID_W, _ffn_rows_per_tile(rows), alpha,
                       batch_major_out=(l == depth - 1))
        if l < depth - 1:
            cs = _conv_ffn(cs_mid, mod_ctx, p, ctx_len, 1, alpha)
    return xs
```

```python
import functools

import jax
import jax.numpy as jnp
from jax import lax
from jax.experimental import pallas as pl
from jax.experimental.pallas import tpu as pltpu

F32 = jnp.float32
BF16 = jnp.bfloat16

LN_EPS = 1e-5
RG_C = 8.0
N_MOD = 6
LATENT_GRID_W = 64
CONV_HALO = 16
VMEM_LIMIT_BYTES = 56 * 1024 * 1024
LANES = 128
SUBLANES = 8


def _cparams(*sem):
    return pltpu.CompilerParams(dimension_semantics=sem, vmem_limit_bytes=VMEM_LIMIT_BYTES)


def _const_spec(shape):
    nd = len(shape)
    return pl.BlockSpec(shape, lambda *_: (0,) * nd)


def _resident_spec(shape):
    nd = len(shape)
    return pl.BlockSpec(shape, lambda *_: (0,) * nd, pipeline_mode=pl.Buffered(1))


def _layer_norm(v, g, b):
    mu = jnp.mean(v, axis=-1, keepdims=True)
    d = v - mu
    var = jnp.mean(d * d, axis=-1, keepdims=True)
    return d * lax.rsqrt(var + LN_EPS) * g + b


def _sigmoid(v):
    return 0.5 + 0.5 * jnp.tanh(0.5 * v)


def _split_bf16(v):
    hi = v.astype(BF16)
    lo = (v - hi.astype(F32)).astype(BF16)
    return hi, lo


def _ada_kernel(c_ref, w_ref, b_ref, o_ref):
    c = c_ref[...]
    s = c * jax.nn.sigmoid(c)
    s_hi, s_lo = _split_bf16(s)
    w_hi, w_lo = _split_bf16(w_ref[0])
    acc = jnp.dot(s_hi, w_hi, preferred_element_type=F32)
    acc += jnp.dot(s_hi, w_lo, preferred_element_type=F32)
    acc += jnp.dot(s_lo, w_hi, preferred_element_type=F32)
    o_ref[0] = acc + b_ref[0]


def _ada_rows(cc, w_ada, b_ada):
    n_layers, d, nc = w_ada.shape
    rows = cc.shape[0]
    tn = 1536
    return pl.pallas_call(
        _ada_kernel,
        name="ada_rows",
        grid=(n_layers, nc // tn),
        in_specs=[
            pl.BlockSpec((rows, d), lambda l, j: (0, 0)),
            pl.BlockSpec((1, d, tn), lambda l, j: (l, 0, j)),
            pl.BlockSpec((1, 1, tn), lambda l, j: (l, 0, j)),
        ],
        out_specs=pl.BlockSpec((1, rows, tn), lambda l, j: (l, 0, j)),
        out_shape=jax.ShapeDtypeStruct((n_layers, rows, nc), F32),
        compiler_params=_cparams("arbitrary", "arbitrary"),
    )(cc, w_ada, b_ada.reshape(n_layers, 1, nc))


def _inproj_kernel(x_ref, shift_ref, scale_ref, w_ref, u_ref, rx_ref, gg_ref):
    n_pos, bsz, d = x_ref.shape
    h = x_ref[...] * (1.0 + scale_ref[...])[None, :, :] + shift_ref[...][None, :, :]
    h = h.reshape(n_pos * bsz, d).astype(BF16)
    z = jnp.dot(h, w_ref[...], preferred_element_type=F32)
    cw = z.shape[1] // 4
    tile = lambda v: v.reshape(n_pos, bsz, cw)
    u_ref[...] = tile(z[:, :cw] * jax.nn.sigmoid(z[:, cw:2 * cw]))
    rx_ref[...] = tile(z[:, 2 * cw:3 * cw])
    gg_ref[...] = tile(jax.nn.gelu(z[:, 3 * cw:]))


def _in_proj(x3, shift, scale, w_in_bf, n_pos):
    t, bsz, d = x3.shape
    cw = w_in_bf.shape[1] // 4
    tile = pl.BlockSpec((n_pos, bsz, cw), lambda i: (i, 0, 0))
    return pl.pallas_call(
        _inproj_kernel,
        name="in_proj",
        grid=(t // n_pos,),
        in_specs=[
            pl.BlockSpec((n_pos, bsz, d), lambda i: (i, 0, 0)),
            _const_spec((bsz, d)), _const_spec((bsz, d)),
            _const_spec(w_in_bf.shape),
        ],
        out_specs=[tile] * 3,
        out_shape=[jax.ShapeDtypeStruct((t, bsz, cw), F32)] * 3,
        compiler_params=_cparams("arbitrary"),
    )(x3, shift, scale, w_in_bf)


def _rows(ref):
    v = ref[...]
    return v.reshape(v.shape[0] * v.shape[1], v.shape[2])


def _short_conv(prev_ref, main_ref, next_ref, w_ref, b_ref, first, last):
    main = _rows(main_ref)
    prev = jnp.where(first, 0.0, _rows(prev_ref))
    nxt = jnp.where(last, 0.0, _rows(next_ref))
    ext = jnp.concatenate([prev, main, nxt], axis=0)
    n = main.shape[0]
    w = w_ref[...]
    acc = b_ref[...] + w[0:1] * ext[0:n]
    for k in range(1, w.shape[0]):
        acc = acc + w[k:k + 1] * ext[SUBLANES * k:SUBLANES * k + n]
    return acc


def _rglru_terms(xr, wax_ref, ba_ref, bx_ref, lam_ref):
    n_blk, bw, _ = wax_ref.shape
    xb = xr.astype(BF16)
    gs = [jnp.dot(xb[:, j * bw:(j + 1) * bw], wax_ref[j], preferred_element_type=F32) for j in range(n_blk)]
    ga = jnp.concatenate([g[:, :bw] for g in gs], axis=1)
    gx = jnp.concatenate([g[:, bw:] for g in gs], axis=1)
    r = _sigmoid(ga + ba_ref[...])
    i = _sigmoid(gx + bx_ref[...])
    nl = -lam_ref[...]
    softplus = jnp.maximum(nl, 0.0) + jnp.log1p(jnp.exp(-jnp.abs(nl)))
    log_a = -RG_C * r * softplus
    a = jnp.exp(log_a)
    th = jnp.tanh(log_a)
    u = jnp.sqrt(-2.0 * th / (1.0 - th)) * (i * xr)
    return a, u


def _scan_positions(a_scr, u_scr, write_h, h_start, n_pos, reverse):
    def body(j, h):
        p = (n_pos - 1 - j) if reverse else j
        rows = pl.ds(pl.multiple_of(p * SUBLANES, SUBLANES), SUBLANES)
        h = a_scr[rows, :] * h + u_scr[rows, :]
        write_h(p, rows, h)
        return h

    return lax.fori_loop(0, n_pos, body, h_start, unroll=4)


def _bwd_scan_kernel(rxp_ref, rxm_ref, rxn_ref, h0_ref, cw_ref, cb_ref, wax_ref, ba_ref, bx_ref, lam_ref,
                     hb_ref, hfin_ref, a_scr, u_scr, carry_scr):
    step = pl.program_id(0)
    n_t = pl.num_programs(0)
    tile = n_t - 1 - step
    n_pos = rxm_ref.shape[0]

    @pl.when(step == 0)
    def _():
        carry_scr[...] = h0_ref[...]

    xr = _short_conv(rxp_ref, rxm_ref, rxn_ref, cw_ref, cb_ref, tile == 0, tile == n_t - 1)
    a, u = _rglru_terms(xr, wax_ref, ba_ref, bx_ref, lam_ref)
    a_scr[...] = a
    u_scr[...] = u

    def write_h(p, rows, h):
        hb_ref[p] = h

    h = _scan_positions(a_scr, u_scr, write_h, carry_scr[...], n_pos, reverse=True)
    carry_scr[...] = h
    hfin_ref[...] = h


def _halo_specs(n_pos, halo_prev, halo_next, t_total, tile_of, tail):
    prev = pl.BlockSpec((halo_prev,) + tail,
                        lambda i: (jnp.maximum(tile_of(i) * (n_pos // halo_prev) - 1, 0), 0, 0))
    nxt = pl.BlockSpec((halo_next,) + tail,
                       lambda i: (jnp.minimum((tile_of(i) + 1) * (n_pos // halo_next), t_total // halo_next - 1),
                                  0, 0))
    return prev, nxt


def _bwd_scan(rx3, h0, p, n_pos):
    t, bsz, cw = rx3.shape
    n_t = t // n_pos
    rev = lambda i: n_t - 1 - i
    prev, nxt = _halo_specs(n_pos, 2, 1, t, rev, (bsz, cw))
    row = _const_spec((1, cw))
    state = _const_spec((bsz, cw))
    return pl.pallas_call(
        _bwd_scan_kernel,
        name="bwd_scan",
        grid=(n_t,),
        in_specs=[
            prev, pl.BlockSpec((n_pos, bsz, cw), lambda i: (rev(i), 0, 0)), nxt, state,
            _const_spec(p["rg_conv_w"].shape), row, _const_spec(p["wax_b"].shape), row, row, row,
        ],
        out_specs=[pl.BlockSpec((n_pos, bsz, cw), lambda i: (rev(i), 0, 0)), state],
        out_shape=[jax.ShapeDtypeStruct((t, bsz, cw), F32), jax.ShapeDtypeStruct((bsz, cw), F32)],
        scratch_shapes=[pltpu.VMEM((n_pos * bsz, cw), F32), pltpu.VMEM((n_pos * bsz, cw), F32),
                        pltpu.VMEM((bsz, cw), F32)],
        compiler_params=_cparams("arbitrary"),
    )(rx3, rx3, rx3, h0, p["rg_conv_w"], p["rg_conv_b"], p["wax_b"], p["ba_b"], p["bx_b"], p["lam_b"])


CONV_STRIP = 64


def _mix_out_kernel(x_ref, g1_ref, up_ref, um_ref, un_ref, rxp_ref, rxm_ref, rxn_ref, gg_ref, hb_ref, h0_ref,
                    cvw_ref, cvb_ref, cvg_ref, cvbeta_ref, cw_ref, cb_ref, wax_ref, ba_ref, bx_ref, lam_ref,
                    wo_ref, lng_ref, lnb_ref, o_ref, hfin_ref, ext_scr, cv_scr, a_scr, u_scr, hf_scr, carry_scr,
                    *, alpha):
    i = pl.program_id(0)
    n_t = pl.num_programs(0)
    first = i == 0
    last = i == n_t - 1
    n_pos, bsz, cw = um_ref.shape
    n_rows = n_pos * bsz
    halo_rows = CONV_HALO * bsz

    @pl.when(first)
    def _():
        carry_scr[...] = h0_ref[...]

    ext_prev = jnp.where(first, 0.0, _rows(up_ref))
    ext_main = _rows(um_ref)
    ext_next = jnp.where(last, 0.0, _rows(un_ref))
    for j in range(cw // LANES):
        ext_scr[j, 0:halo_rows, :] = ext_prev[:, j * LANES:(j + 1) * LANES]
        ext_scr[j, halo_rows:halo_rows + n_rows, :] = ext_main[:, j * LANES:(j + 1) * LANES]
        ext_scr[j, halo_rows + n_rows:, :] = ext_next[:, j * LANES:(j + 1) * LANES]
    n_taps = cvw_ref.shape[0]
    off = (CONV_HALO - (n_taps - 1) // 2) * bsz

    for j in range(cw // LANES):
        lanes = pl.ds(j * LANES, LANES)

        def conv_strip(s, carry, j=j, lanes=lanes):
            base = pl.multiple_of(s * CONV_STRIP, CONV_STRIP)
            parts = [None] * 4
            for k in range(n_taps):
                term = cvw_ref[k:k + 1, lanes] * ext_scr[j, pl.ds(base + off + k * bsz, CONV_STRIP), :]
                parts[k % 4] = term if parts[k % 4] is None else parts[k % 4] + term
            cv_scr[pl.ds(base, CONV_STRIP), lanes] = ((parts[0] + parts[1]) + (parts[2] + parts[3])) + cvb_ref[:, lanes]
            return carry

        lax.fori_loop(0, n_rows // CONV_STRIP, conv_strip, 0)
    v = _layer_norm(cv_scr[...], cvg_ref[...], cvbeta_ref[...])
    y_cv = v * jax.nn.sigmoid(v)

    xr = _short_conv(rxp_ref, rxm_ref, rxn_ref, cw_ref, cb_ref, first, last)
    a, u = _rglru_terms(xr, wax_ref, ba_ref, bx_ref, lam_ref)
    a_scr[...] = a
    u_scr[...] = u

    def write_h(p, rows, h):
        hf_scr[rows, :] = h

    h = _scan_positions(a_scr, u_scr, write_h, carry_scr[...], n_pos, reverse=False)
    carry_scr[...] = h
    hfin_ref[...] = h
    y_rg = (hf_scr[...] + _rows(hb_ref)) * _rows(gg_ref)

    m = jnp.dot(y_cv.astype(BF16), wo_ref[0:cw, :], preferred_element_type=F32)
    m += jnp.dot(y_rg.astype(BF16), wo_ref[cw:, :], preferred_element_type=F32)
    d = m.shape[1]
    gate = g1_ref[...]
    v = alpha * x_ref[...] + gate[None, :, :] * m.reshape(n_pos, bsz, d)
    o_ref[...] = _layer_norm(v, lng_ref[...], lnb_ref[...])


def _mix_out(x3, g1, u3, rx3, gg3, hb3, h0, p, n_pos, alpha):
    t, bsz, d = x3.shape
    cw = rx3.shape[2]
    n_t = t // n_pos
    ident = lambda i: i
    up, un = _halo_specs(n_pos, CONV_HALO, CONV_HALO, t, ident, (bsz, cw))
    rxp, rxn = _halo_specs(n_pos, 2, 1, t, ident, (bsz, cw))
    tile = pl.BlockSpec((n_pos, bsz, cw), lambda i: (i, 0, 0))
    xtile = pl.BlockSpec((n_pos, bsz, d), lambda i: (i, 0, 0))
    state = _const_spec((bsz, cw))
    row = _const_spec((1, cw))
    drow = _const_spec((1, d))
    n_rows = n_pos * bsz
    return pl.pallas_call(
        functools.partial(_mix_out_kernel, alpha=alpha),
        name="mix_out",
        grid=(n_t,),
        in_specs=[
            xtile, _const_spec((bsz, d)),
            up, tile, un, rxp, tile, rxn, tile, tile, state,
            _const_spec(p["conv_w"].shape), row, row, row,
            _const_spec(p["rg_conv_w"].shape), row, _const_spec(p["wax_f"].shape), row, row, row,
            _const_spec(p["w_out"].shape), drow, drow,
        ],
        out_specs=[xtile, state],
        out_shape=[jax.ShapeDtypeStruct((t, bsz, d), F32), jax.ShapeDtypeStruct((bsz, cw), F32)],
        scratch_shapes=[
            pltpu.VMEM((cw // LANES, n_rows + 2 * CONV_HALO * bsz, LANES), F32),
            pltpu.VMEM((n_rows, cw), F32),
            pltpu.VMEM((n_rows, cw), F32), pltpu.VMEM((n_rows, cw), F32),
            pltpu.VMEM((n_rows, cw), F32),
            pltpu.VMEM((bsz, cw), F32),
        ],
        compiler_params=_cparams("arbitrary"),
    )(x3, g1, u3, u3, u3, rx3, rx3, rx3, gg3, hb3, h0,
      p["conv_w"], p["conv_b"], p["conv_ln_g"], p["conv_ln_b"],
      p["rg_conv_w"], p["rg_conv_b"], p["wax_f"], p["ba_f"], p["bx_f"], p["lam_f"],
      p["w_out"], p["ln1_g"], p["ln1_b"])


def _grid_conv_row(z_ref, col0, row0, width, w9, bias, multi_row):
    lanes = pl.ds(col0, LANES)
    if multi_row:
        zt = z_ref[pl.ds(row0, width), lanes]
        zm = z_ref[pl.ds(row0 + width, width), lanes]
        zb = z_ref[pl.ds(row0 + 2 * width, width), lanes]
        v = [(w9[dj] * zt + w9[3 + dj] * zm + w9[6 + dj] * zb).astype(F32) for dj in range(3)]
    else:
        zm = z_ref[pl.ds(row0, width), lanes]
        v = [(w9[3 + dj] * zm).astype(F32) for dj in range(3)]
    pos = lax.broadcasted_iota(jnp.int32, (width, LANES), 0)
    left = jnp.where(pos == 0, 0.0, pltpu.roll(v[0], 1, axis=0))
    right = jnp.where(pos == width - 1, 0.0, pltpu.roll(v[2], width - 1, axis=0))
    return (v[1] + bias) + (left + right)


GELU_K0 = (2.0 / 3.141592653589793) ** 0.5
GELU_K1 = GELU_K0 * 0.044715


def _twice_gelu(g):
    return g + g * jnp.tanh(g * (GELU_K0 + GELU_K1 * (g * g)))


def _ffn_kernel(*refs, width, multi_row, n_groups, alpha):
    if multi_row:
        xp_ref, xm_ref, xn_ref = refs[:3]
        refs = refs[3:]
    else:
        xm_ref = refs[0]
        refs = refs[1:]
    mod_ref, wu_ref, dw_ref, db_ref, wd_ref, lng_ref, lnb_ref, o_ref, h_scr, za_scr, zb_scr, y_scr = refs
    n_out = xm_ref.shape[0]
    fc = db_ref.shape[2]
    n_f = wd_ref.shape[0] // fc
    off = width if multi_row else 0

    shift = mod_ref[0, 3:4, :]
    scale = mod_ref[0, 4:5, :]
    h_scr[off:off + n_out] = (xm_ref[...] * (1.0 + scale) + shift).astype(BF16)
    if multi_row:
        r = pl.program_id(1)
        n_r = pl.num_programs(1)
        hp = xp_ref[...] * (1.0 + scale) + shift
        hn = xn_ref[...] * (1.0 + scale) + shift
        h_scr[0:width] = jnp.where(r == 0, 0.0, hp).astype(BF16)
        h_scr[off + n_out:] = jnp.where(r == n_r - 1, 0.0, hn).astype(BF16)

    m_ext = h_scr.shape[0]
    up_rows = m_ext // n_groups
    out_rows = n_out // n_groups

    z_bufs = (za_scr, zb_scr)

    def up_project(f, slot, g):
        z_scr = z_bufs[slot]
        rows = pl.ds(pl.multiple_of(g * up_rows, 16), up_rows)
        h = h_scr[rows, :]
        z_scr[rows, 0:fc] = jnp.dot(h, wu_ref[f], preferred_element_type=F32).astype(BF16)
        z_scr[rows, fc:2 * fc] = jnp.dot(h, wu_ref[n_f + f], preferred_element_type=F32).astype(BF16)

    def conv_gate(f, slot, g):
        z_scr = z_bufs[slot]
        base = pl.multiple_of(g * out_rows, out_rows)
        for cb in range(0, fc, LANES):
            wu9 = [dw_ref[f, k:k + 1, cb:cb + LANES].astype(BF16) for k in range(9)]
            wg9 = [dw_ref[n_f + f, k:k + 1, cb:cb + LANES].astype(BF16) for k in range(9)]
            bu = db_ref[f, :, cb:cb + LANES]
            bg = db_ref[n_f + f, :, cb:cb + LANES]
            for r0 in range(0, out_rows, width):
                cu = _grid_conv_row(z_scr, cb, base + r0, width, wu9, bu, multi_row)
                cg = _grid_conv_row(z_scr, fc + cb, base + r0, width, wg9, bg, multi_row)
                cols = pl.ds(pl.multiple_of(f * fc, fc) + cb, LANES)
                y_scr[pl.ds(base + r0, width), cols] = (cu * _twice_gelu(cg)).astype(BF16)

    def groups(body):
        lax.fori_loop(0, n_groups, lambda g, c: (body(g), c)[1], 0)

    groups(lambda g: up_project(0, 0, g))

    def stage(f, slot):
        def body(g):
            up_project(f + 1, 1 - slot, g)
            conv_gate(f, slot, g)
        groups(body)

    def chunk_pair(j, carry):
        stage(2 * j, 0)
        stage(2 * j + 1, 1)
        return carry

    lax.fori_loop(0, (n_f - 1) // 2, chunk_pair, 0)
    if (n_f - 1) % 2:
        stage(n_f - 2, 0)
    groups(lambda g: conv_gate(n_f - 1, (n_f - 1) % 2, g))

    ffn = jnp.dot(y_scr[...], wd_ref[...], preferred_element_type=F32)
    gate = mod_ref[0, 5:6, :]
    out = _layer_norm(alpha * xm_ref[...] + gate * ffn, lng_ref[...], lnb_ref[...])
    o_ref[...] = out.reshape(o_ref.shape)


def _conv_ffn(x2, mod, p, width, rows_per_tile, alpha, batch_major_out=False):
    bsz, _, d = mod.shape
    t = x2.shape[0]
    n_out = rows_per_tile * width
    n_r = t // n_out
    multi_row = t > width
    fc = p["ffn_dw_b"].shape[2]
    last_row = t // width - 1
    main = pl.BlockSpec((n_out, d), lambda b, r: (r, b))
    x_specs = [main]
    x_args = [x2]
    if multi_row:
        x_specs = [
            pl.BlockSpec((width, d), lambda b, r: (jnp.maximum(r * rows_per_tile - 1, 0), b)),
            main,
            pl.BlockSpec((width, d), lambda b, r: (jnp.minimum((r + 1) * rows_per_tile, last_row), b)),
        ]
        x_args = [x2, x2, x2]
    m_ext = n_out + (2 * width if multi_row else 0)
    n_groups = _ffn_row_groups(rows_per_tile, m_ext)
    drow = _const_spec((1, d))
    return pl.pallas_call(
        functools.partial(_ffn_kernel, width=width, multi_row=multi_row, n_groups=n_groups, alpha=alpha),
        name="conv_ffn",
        grid=(bsz, n_r),
        in_specs=x_specs + [
            pl.BlockSpec((1, N_MOD, d), lambda b, r: (b, 0, 0)),
            _resident_spec(p["ffn_up"].shape), _resident_spec(p["ffn_dw"].shape),
            _resident_spec(p["ffn_dw_b"].shape), _resident_spec(p["ffn_down"].shape),
            drow, drow,
        ],
        out_specs=pl.BlockSpec((1, n_out, d), lambda b, r: (b, r, 0)) if batch_major_out else main,
        out_shape=jax.ShapeDtypeStruct((bsz, t, d) if batch_major_out else (t, bsz * d), F32),
        scratch_shapes=[
            pltpu.VMEM((m_ext, d), BF16),
            pltpu.VMEM((m_ext, 2 * fc), BF16),
            pltpu.VMEM((m_ext, 2 * fc), BF16),
            pltpu.VMEM((n_out, p["ffn_down"].shape[0]), BF16),
        ],
        compiler_params=_cparams("arbitrary", "arbitrary"),
    )(*x_args, mod, p["ffn_up"], p["ffn_dw"], p["ffn_dw_b"], p["ffn_down"], p["ln2_g"], p["ln2_b"])


def _in_proj_positions(t):
    return 128 if t % 128 == 0 else t


def _seq_positions(t):
    return 64 if t % 64 == 0 else t


def _ffn_rows_per_tile(rows):
    return 16 if rows % 16 == 0 else rows


def _ffn_row_groups(rows_per_tile, m_ext):
    n = max(rows_per_tile // 16, 1)
    while m_ext % (16 * n) or rows_per_tile % n:
        n -= 1
    return n


def _ffn_chunk(f_dim):
    return 256 if f_dim % 256 == 0 else 128


GATE_BLOCK = 256


def _block_diag_blocks(w):
    nh, dh, _ = w.shape
    per = GATE_BLOCK // dh
    eye = jnp.eye(per, dtype=w.dtype)
    wb = w.reshape(nh // per, per, dh, dh)
    return jnp.einsum("bhij,hg->bhigj", wb, eye).reshape(nh // per, per * dh, per * dh)


def _layer_params(l, w_in, conv_w, conv_b, conv_ln_g, conv_ln_b, rg_conv_w, rg_conv_b, rg_wa, rg_ba, rg_wx,
                  rg_bx, rg_lambda, w_out, ln1_g, ln1_b, ffn_up, ffn_dw, ffn_dw_b, ffn_down, ln2_g, ln2_b):
    row = lambda v: v.reshape(1, -1)
    p = {
        "w_in": w_in[l].astype(BF16),
        "conv_w": conv_w[l], "conv_b": row(conv_b[l]),
        "conv_ln_g": row(conv_ln_g[l]), "conv_ln_b": row(conv_ln_b[l]),
        "rg_conv_w": rg_conv_w[l], "rg_conv_b": row(rg_conv_b[l]),
        "w_out": w_out[l].astype(BF16), "ln1_g": row(ln1_g[l]), "ln1_b": row(ln1_b[l]),
        "ln2_g": row(ln2_g[l]), "ln2_b": row(ln2_b[l]),
    }
    f_dim, d = ffn_down.shape[1:]
    fc = _ffn_chunk(f_dim)
    n_c = 2 * f_dim // fc
    p["ffn_up"] = ffn_up[l].astype(BF16).reshape(d, n_c, fc).transpose(1, 0, 2)
    half_u = jnp.where(jnp.arange(n_c) < n_c // 2, 0.5, 1.0).astype(F32)[:, None, None]
    p["ffn_dw"] = ffn_dw[l].reshape(9, n_c, fc).transpose(1, 0, 2) * half_u
    p["ffn_dw_b"] = ffn_dw_b[l].reshape(n_c, 1, fc) * half_u
    p["ffn_down"] = ffn_down[l].astype(BF16)
    for k, name in ((0, "f"), (1, "b")):
        p["wax_" + name] = jnp.concatenate(
            [_block_diag_blocks(rg_wa[l, k]), _block_diag_blocks(rg_wx[l, k])], axis=2).astype(BF16)
        p["ba_" + name] = row(rg_ba[l, k])
        p["bx_" + name] = row(rg_bx[l, k])
        p["lam_" + name] = row(rg_lambda[l, k])
    return p


def _mixer(x2, mod, p, h0_f, h0_b, alpha):
    bsz, _, d = mod.shape
    t = x2.shape[0]
    x3 = x2.reshape(t, bsz, d)
    n_pos = _seq_positions(t)
    u3, rx3, gg3 = _in_proj(x3, mod[:, 0, :], mod[:, 1, :], p["w_in"], _in_proj_positions(t))
    hb3, hb_fin = _bwd_scan(rx3, h0_b, p, n_pos)
    x_new, hf_fin = _mix_out(x3, mod[:, 2, :], u3, rx3, gg3, hb3, h0_f, p, n_pos, alpha)
    return x_new.reshape(t, bsz * d), hf_fin, hb_fin


def kernel(x, c, ctx, c_ctx, w_ada, b_ada, w_in, conv_w, conv_b, conv_ln_g, conv_ln_b, rg_conv_w, rg_conv_b,
           rg_wa, rg_ba, rg_wx, rg_bx, rg_lambda, w_out, ln1_g, ln1_b, ffn_up, ffn_dw, ffn_dw_b, ffn_down,
           ln2_g, ln2_b):
    bsz, seq, d = x.shape
    ctx_len = ctx.shape[1]
    depth = w_in.shape[0]
    alpha = (2.0 * depth) ** 0.25
    cw = rg_lambda.shape[-1]
    assert bsz == SUBLANES, "the sequence kernels put the batch on the 8 sublanes of a vreg"

    pad = (-(bsz + 1)) % 8
    cc = jnp.concatenate([c, c_ctx[None, :], jnp.zeros((pad, d), F32)], axis=0)
    mods = _ada_rows(cc, w_ada, b_ada).reshape(depth, bsz + 1 + pad, N_MOD, d)
    zeros_state = jnp.zeros((bsz, cw), F32)

    xs = jnp.transpose(x, (1, 0, 2)).reshape(seq, bsz * d)
    cs = jnp.transpose(ctx, (1, 0, 2)).reshape(ctx_len, bsz * d)
    rows = seq // LATENT_GRID_W

    for l in range(depth):
        p = _layer_params(l, w_in, conv_w, conv_b, conv_ln_g, conv_ln_b, rg_conv_w, rg_conv_b, rg_wa, rg_ba,
                          rg_wx, rg_bx, rg_lambda, w_out, ln1_g, ln1_b, ffn_up, ffn_dw, ffn_dw_b, ffn_down,
                          ln2_g, ln2_b)
        mod_lat = mods[l, :bsz]
        mod_ctx = jnp.broadcast_to(mods[l, bsz:bsz + 1], (bsz, N_MOD, d))
        cs_mid, hc_f, hc_b = _mixer(cs, mod_ctx, p, zeros_state, zeros_state, alpha)
        xs, _, _ = _mixer(xs, mod_lat, p, hc_f, hc_b, alpha)
        xs = _conv_ffn(xs, mod_lat, p, LATENT_GRID_W, _ffn_rows_per_tile(rows), alpha,
                       batch_major_out=(l == depth - 1))
        if l < depth - 1:
            cs = _conv_ffn(cs_mid, mod_ctx, p, ctx_len, 1, alpha)
    return xs
```
